```python
import jax
import jax.numpy as jnp
from jax import lax
import numpy as np

D_MODEL = 1024
BATCH = 8
SEQ = 2048
DEPTH = 4

N_MEM = 256
EPS = 1e-6
CHUNK = 64
CONV_W = 4
GLA_HEADS = 4
GLA_DK = 64
GLA_DV = 128
GLA_KEY = GLA_HEADS * GLA_DK
GLA_VAL = GLA_HEADS * GLA_DV
GLA_GATE_RANK = 16
GLA_GATE_NORM = 16.0
LRU_WIDTH = D_MODEL // 2
LRU_BLOCKS = 4
LRU_BLOCK = LRU_WIDTH // LRU_BLOCKS
LRU_C = 8.0
AB_IN = 2 * GLA_KEY + 2 * GLA_VAL + GLA_GATE_RANK + 2 * LRU_WIDTH
AB_MIX = GLA_VAL + LRU_WIDTH
GDN_HEADS = 8
GDN_DK = 128
GDN_DV = 128
GDN_KEY = GDN_HEADS * GDN_DK
GDN_VAL = GDN_HEADS * GDN_DV
GDN_CONV_CH = 2 * GDN_KEY + GDN_VAL
GDN_IN = GDN_CONV_CH + GDN_VAL + 2 * GDN_HEADS
XA_HEADS = 4
XA_DH = D_MODEL // XA_HEADS
D_FF = 4 * D_MODEL
N_EVEN = (DEPTH + 1) // 2
N_ODD = DEPTH // 2

kernel_name = "hybrid_gla_rglru_gdn_memxattn_block"

F32 = jnp.float32


def rms_norm(x, g):
    xf = x.astype(F32)
    y = xf * lax.rsqrt(jnp.mean(xf * xf, axis=-1, keepdims=True) + EPS)
    return (y * g.astype(F32)).astype(x.dtype)


def l2_norm(x):
    xf = x.astype(F32)
    return xf * lax.rsqrt(jnp.sum(xf * xf, axis=-1, keepdims=True) + EPS)


def causal_dwconv(x, w, b=None):
    K = w.shape[0]
    T = x.shape[1]
    xp = jnp.pad(x, ((0, 0), (K - 1, 0), (0, 0)))
    y = xp[:, 0:T] * w[0]
    for kk in range(1, K):
        y = y + xp[:, kk:kk + T] * w[kk]
    if b is not None:
        y = y + b
    return y


def gla_chunked(q, k, v, gk):
    B, T, H, dk = q.shape
    dv = v.shape[-1]
    N = T // CHUNK

    def to_chunks(t):
        return t.astype(F32).reshape(B, N, CHUNK, H, t.shape[-1]).transpose(1, 0, 3, 2, 4)

    qc_all = to_chunks(q) * (dk ** -0.5)
    kc_all = to_chunks(k)
    vc_all = to_chunks(v)
    bc_all = jnp.cumsum(to_chunks(gk), axis=-2)
    causal = jnp.tril(jnp.ones((CHUNK, CHUNK), dtype=bool))

    def step(S, inp):
        qc, kc, vc, bc = inp
        diff = bc[..., :, None, :] - bc[..., None, :, :]
        decay = jnp.exp(jnp.where(causal[:, :, None], diff, -jnp.inf))
        A = jnp.einsum('bhid,bhijd,bhjd->bhij', qc, decay, kc)
        o = jnp.einsum('bhij,bhjv->bhiv', A, vc) + jnp.einsum('bhid,bhdv->bhiv', qc * jnp.exp(bc), S)
        blast = bc[..., -1:, :]
        S = S * jnp.exp(blast[..., 0, :])[..., None] + jnp.einsum('bhjd,bhjv->bhdv', kc * jnp.exp(blast - bc), vc)
        return S, o

    S0 = jnp.zeros((B, H, dk, dv), F32)
    _, o = lax.scan(step, S0, (qc_all, kc_all, vc_all, bc_all))
    return o.transpose(1, 0, 3, 2, 4).reshape(B, T, H, dv)


def rglru(x, w_a, b_a, w_x, b_x, lam):
    B, T, W = x.shape
    xf = x.astype(F32)
    xb = xf.reshape(B, T, LRU_BLOCKS, LRU_BLOCK)
    r = jax.nn.sigmoid(jnp.einsum('btni,nij->btnj', xb, w_a.astype(F32)).reshape(B, T, W) + b_a.astype(F32))
    i_g = jax.nn.sigmoid(jnp.einsum('btni,nij->btnj', xb, w_x.astype(F32)).reshape(B, T, W) + b_x.astype(F32))
    log_a = -LRU_C * r * jax.nn.softplus(-lam.astype(F32))
    a = jnp.exp(log_a)
    u = jnp.sqrt(-jnp.expm1(2.0 * log_a)) * (i_g * xf)

    def combine(left, right):
        return (left[0] * right[0], right[0] * left[1] + right[1])

    _, h = lax.associative_scan(combine, (a, u), axis=1)
    return h


def gated_delta_chunked(q, k, v, g, beta):
    B, T, H, dk = q.shape
    dv = v.shape[-1]
    N = T // CHUNK

    def to_chunks(t):
        return t.astype(F32).reshape(B, N, CHUNK, H, t.shape[-1]).transpose(0, 3, 1, 2, 4)

    qc = to_chunks(q) * (dk ** -0.5)
    kc = to_chunks(k)
    vc = to_chunks(v)
    gcum = jnp.cumsum(g.astype(F32).reshape(B, N, CHUNK, H).transpose(0, 3, 1, 2), axis=-1)
    bt = beta.astype(F32).reshape(B, N, CHUNK, H).transpose(0, 3, 1, 2)[..., None]
    incl = jnp.tril(jnp.ones((CHUNK, CHUNK), dtype=bool))
    strict = jnp.tril(jnp.ones((CHUNK, CHUNK), dtype=bool), k=-1)
    L = jnp.exp(jnp.where(incl, gcum[..., :, None] - gcum[..., None, :], -jnp.inf))
    kb = kc * bt
    vb = vc * bt
    M = jnp.where(strict, jnp.einsum('bhnid,bhnjd->bhnij', kb, kc) * L, 0.0)
    IM = M + jnp.eye(CHUNK, dtype=F32)
    rhs = jnp.concatenate([vb, kb * jnp.exp(gcum)[..., None]], axis=-1)
    sol = lax.linalg.triangular_solve(IM, rhs, left_side=True, lower=True, unit_diagonal=True)
    u_all = sol[..., :dv]
    w_all = sol[..., dv:]
    Aqk = jnp.where(incl, jnp.einsum('bhnid,bhnjd->bhnij', qc, kc) * L, 0.0)

    def nfirst(t):
        return jnp.moveaxis(t, 2, 0)

    def step(S, inp):
        qn, kn, un, wn, an, gn = inp
        v_new = un - jnp.einsum('bhcd,bhdv->bhcv', wn, S)
        o = jnp.einsum('bhcd,bhdv->bhcv', qn * jnp.exp(gn)[..., None], S) + jnp.einsum('bhij,bhjv->bhiv', an, v_new)
        glast = gn[..., -1]
        S = S * jnp.exp(glast)[..., None, None] + jnp.einsum('bhcd,bhcv->bhdv', kn * jnp.exp(glast[..., None] - gn)[..., None], v_new)
        return S, o

    S0 = jnp.zeros((B, H, dk, dv), F32)
    _, o = lax.scan(step, S0, (nfirst(qc), nfirst(kc), nfirst(u_all), nfirst(w_all), nfirst(Aqk), nfirst(gcum)))
    return o.transpose(1, 0, 3, 2, 4).reshape(B, T, H, dv)


def gla_lru_mixer(h, w_in, gk_w2, gk_b, gla_g, conv_w, conv_b, w_a, b_a, w_x, b_x, lam, w_out):
    B, T, _ = h.shape
    proj = h @ w_in
    o1 = GLA_KEY
    o2 = o1 + GLA_KEY
    o3 = o2 + GLA_VAL
    o4 = o3 + GLA_VAL
    o5 = o4 + GLA_GATE_RANK
    o6 = o5 + LRU_WIDTH
    q, k, v, g_out, gk_lr, lru_x, lru_gate = jnp.split(proj, [o1, o2, o3, o4, o5, o6], axis=-1)
    gk = jax.nn.log_sigmoid((gk_lr @ gk_w2 + gk_b).astype(F32)) / GLA_GATE_NORM
    o = gla_chunked(q.reshape(B, T, GLA_HEADS, GLA_DK), k.reshape(B, T, GLA_HEADS, GLA_DK),
                    v.reshape(B, T, GLA_HEADS, GLA_DV), gk.reshape(B, T, GLA_HEADS, GLA_DK))
    o = rms_norm(o, gla_g) * jax.nn.silu(g_out.reshape(B, T, GLA_HEADS, GLA_DV).astype(F32))
    xr = causal_dwconv(lru_x, conv_w, conv_b)
    y = rglru(xr, w_a, b_a, w_x, b_x, lam) * jax.nn.gelu(lru_gate.astype(F32))
    mixed = jnp.concatenate([o.reshape(B, T, GLA_VAL), y], axis=-1).astype(h.dtype)
    return mixed @ w_out


def gdn_mixer(h, w_in, conv_w, a_log, dt_bias, norm_g, w_out):
    B, T, _ = h.shape
    proj = h @ w_in
    p1 = GDN_CONV_CH
    p2 = p1 + GDN_VAL
    p3 = p2 + GDN_HEADS
    qkv, z, b_logit, a_logit = jnp.split(proj, [p1, p2, p3], axis=-1)
    qkv = jax.nn.silu(causal_dwconv(qkv, conv_w))
    q, k, v = jnp.split(qkv, [GDN_KEY, 2 * GDN_KEY], axis=-1)
    q = l2_norm(q.reshape(B, T, GDN_HEADS, GDN_DK))
    k = l2_norm(k.reshape(B, T, GDN_HEADS, GDN_DK))
    beta = jax.nn.sigmoid(b_logit.astype(F32))
    g = -jnp.exp(a_log.astype(F32)) * jax.nn.softplus(a_logit.astype(F32) + dt_bias.astype(F32))
    o = gated_delta_chunked(q, k, v.reshape(B, T, GDN_HEADS, GDN_DV), g, beta)
    o = rms_norm(o, norm_g) * jax.nn.silu(z.reshape(B, T, GDN_HEADS, GDN_DV).astype(F32))
    return o.reshape(B, T, GDN_VAL).astype(h.dtype) @ w_out


def mem_cross_attn(h, mem_n, w_q, w_kv, w_o):
    B, T, _ = h.shape
    M = mem_n.shape[1]
    q = (h @ w_q).reshape(B, T, XA_HEADS, XA_DH).astype(F32)
    kv = mem_n @ w_kv
    k = kv[..., :D_MODEL].reshape(B, M, XA_HEADS, XA_DH).astype(F32)
    v = kv[..., D_MODEL:].reshape(B, M, XA_HEADS, XA_DH).astype(F32)
    s = jnp.einsum('bthd,bmhd->bhtm', q, k) * (XA_DH ** -0.5)
    p = jax.nn.softmax(s, axis=-1)
    o = jnp.einsum('bhtm,bmhd->bthd', p, v).reshape(B, T, D_MODEL).astype(h.dtype)
    return o @ w_o


def sq_relu_mlp(h, w1, w2):
    return jnp.square(jax.nn.relu(h @ w1)) @ w2


def setup_inputs(seed: int = 0) -> dict:
    key = jax.random.key(seed)
    ks = iter(jax.random.split(key, 64))

    def nrm(shape, fan_in):
        return jax.random.normal(next(ks), shape, F32) * (fan_in ** -0.5)

    def gain(shape):
        return 1.0 + 0.02 * jax.random.normal(next(ks), shape, F32)

    def small(shape):
        return 0.01 * jax.random.normal(next(ks), shape, F32)

    x = jax.random.normal(next(ks), (BATCH, SEQ, D_MODEL), F32)
    mem = jax.random.normal(next(ks), (BATCH, N_MEM, D_MODEL), F32)
    u = jax.random.uniform(next(ks), (N_EVEN, LRU_WIDTH), F32, minval=0.9, maxval=0.999)
    p = u ** (1.0 / LRU_C)
    lru_lambda = jnp.log(p) - jnp.log1p(-p)
    gdn_a_log = jnp.log(jax.random.uniform(next(ks), (N_ODD, GDN_HEADS), F32, minval=1.0, maxval=16.0))
    dt = jnp.exp(jax.random.uniform(next(ks), (N_ODD, GDN_HEADS), F32, minval=float(np.log(1e-3)), maxval=float(np.log(1e-1))))
    gdn_dt_bias = dt + jnp.log(-jnp.expm1(-dt))
    return {
        "x": x,
        "mem": mem,
        "mix_pre_g": gain((DEPTH, D_MODEL)),
        "mix_post_g": gain((DEPTH, D_MODEL)),
        "ab_w_in": nrm((N_EVEN, D_MODEL, AB_IN), D_MODEL),
        "gla_gk_w2": nrm((N_EVEN, GLA_GATE_RANK, GLA_KEY), GLA_GATE_RANK),
        "gla_gk_b": small((N_EVEN, GLA_KEY)),
        "gla_norm_g": gain((N_EVEN, GLA_DV)),
        "lru_conv_w": nrm((N_EVEN, CONV_W, LRU_WIDTH), CONV_W),
        "lru_conv_b": small((N_EVEN, LRU_WIDTH)),
        "lru_w_a": nrm((N_EVEN, LRU_BLOCKS, LRU_BLOCK, LRU_BLOCK), LRU_BLOCK),
        "lru_b_a": small((N_EVEN, LRU_WIDTH)),
        "lru_w_x": nrm((N_EVEN, LRU_BLOCKS, LRU_BLOCK, LRU_BLOCK), LRU_BLOCK),
        "lru_b_x": small((N_EVEN, LRU_WIDTH)),
        "lru_lambda": lru_lambda,
        "ab_w_out": nrm((N_EVEN, AB_MIX, D_MODEL), AB_MIX),
        "gdn_w_in": nrm((N_ODD, D_MODEL, GDN_IN), D_MODEL),
        "gdn_conv_w": nrm((N_ODD, CONV_W, GDN_CONV_CH), CONV_W),
        "gdn_a_log": gdn_a_log,
        "gdn_dt_bias": gdn_dt_bias,
        "gdn_norm_g": gain((N_ODD, GDN_DV)),
        "gdn_w_out": nrm((N_ODD, GDN_VAL, D_MODEL), GDN_VAL),
        "xa_pre_g": gain((DEPTH, D_MODEL)),
        "xa_mem_g": gain((DEPTH, D_MODEL)),
        "xa_post_g": gain((DEPTH, D_MODEL)),
        "xa_w_q": nrm((DEPTH, D_MODEL, D_MODEL), D_MODEL),
        "xa_w_kv": nrm((DEPTH, D_MODEL, 2 * D_MODEL), D_MODEL),
        "xa_w_o": nrm((DEPTH, D_MODEL, D_MODEL), D_MODEL),
        "mlp_pre_g": gain((DEPTH, D_MODEL)),
        "mlp_post_g": gain((DEPTH, D_MODEL)),
        "mlp_w1": nrm((DEPTH, D_MODEL, D_FF), D_MODEL),
        "mlp_w2": nrm((DEPTH, D_FF, D_MODEL), D_FF),
    }


def reference(x, mem, mix_pre_g, mix_post_g, ab_w_in, gla_gk_w2, gla_gk_b, gla_norm_g, lru_conv_w, lru_conv_b,
              lru_w_a, lru_b_a, lru_w_x, lru_b_x, lru_lambda, ab_w_out, gdn_w_in, gdn_conv_w, gdn_a_log, gdn_dt_bias,
              gdn_norm_g, gdn_w_out, xa_pre_g, xa_mem_g, xa_post_g, xa_w_q, xa_w_kv, xa_w_o, mlp_pre_g, mlp_post_g,
              mlp_w1, mlp_w2):
    h = x
    for i in range(DEPTH):
        j = i // 2
        hn = rms_norm(h, mix_pre_g[i])
        if i % 2 == 0:
            m = gla_lru_mixer(hn, ab_w_in[j], gla_gk_w2[j], gla_gk_b[j], gla_norm_g[j], lru_conv_w[j], lru_conv_b[j],
                              lru_w_a[j], lru_b_a[j], lru_w_x[j], lru_b_x[j], lru_lambda[j], ab_w_out[j])
        else:
            m = gdn_mixer(hn, gdn_w_in[j], gdn_conv_w[j], gdn_a_log[j], gdn_dt_bias[j], gdn_norm_g[j], gdn_w_out[j])
        h = h + rms_norm(m, mix_post_g[i])
        c = mem_cross_attn(rms_norm(h, xa_pre_g[i]), rms_norm(mem, xa_mem_g[i]), xa_w_q[i], xa_w_kv[i], xa_w_o[i])
        h = h + rms_norm(c, xa_post_g[i])
        f = sq_relu_mlp(rms_norm(h, mlp_pre_g[i]), mlp_w1[i], mlp_w2[i])
        h = h + rms_norm(f, mlp_post_g[i])
    return h
```

```python
import functools

import numpy as np
import jax
import jax.numpy as jnp
from jax import lax
from jax.experimental import pallas as pl
from jax.experimental.pallas import tpu as pltpu

F32 = jnp.float32
BF16 = jnp.bfloat16

D_MODEL = 1024
N_MEM = 256
EPS = 1e-6
CHUNK = 64
CONV_W = 4
GLA_HEADS = 4
GLA_DK = 64
GLA_DV = 128
GLA_KEY = GLA_HEADS * GLA_DK
GLA_VAL = GLA_HEADS * GLA_DV
GLA_GATE_RANK = 16
GLA_GATE_NORM = 16.0
LRU_WIDTH = D_MODEL // 2
LRU_BLOCKS = 4
LRU_BLOCK = LRU_WIDTH // LRU_BLOCKS
LRU_C = 8.0
GDN_HEADS = 8
GDN_DK = 128
GDN_DV = 128
GDN_KEY = GDN_HEADS * GDN_DK
GDN_VAL = GDN_HEADS * GDN_DV
XA_HEADS = 4
XA_DH = D_MODEL // XA_HEADS
D_FF = 4 * D_MODEL

LANE = 128
VMEM_LIMIT = 56 * 1024 * 1024

GLA_LEVELS = (32, 16, 8, 4, 2, 1)
GDN_LEVELS = (2, 4, 8, 16, 32)

NT_DIMS = (((1,), (1,)), ((), ()))
TN_DIMS = (((0,), (0,)), ((), ()))


def _params(n_axes):
    return pltpu.CompilerParams(
        dimension_semantics=("arbitrary",) * n_axes, vmem_limit_bytes=VMEM_LIMIT)


def _rms(x, g):
    ms = jnp.mean(x * x, axis=-1, keepdims=True)
    return x * lax.rsqrt(ms + EPS) * g


def _softplus(x):
    return jnp.maximum(x, 0.0) + jnp.log1p(jnp.exp(-jnp.abs(x)))


def _sigmoid(x):
    return 1.0 / (1.0 + jnp.exp(-x))


def _silu(x):
    return x * _sigmoid(x)


def _dot(a, b):
    return jnp.dot(a, b, preferred_element_type=F32)


def _dot_nt(a, b):
    return lax.dot_general(a, b, NT_DIMS, preferred_element_type=F32)


def _dot_tn(a, b):
    return lax.dot_general(a, b, TN_DIMS, preferred_element_type=F32)


def _split2(x):
    hi = x.astype(BF16)
    lo = (x - hi.astype(F32)).astype(BF16)
    return hi, lo


def _split3(x):
    hi = x.astype(BF16)
    r = x - hi.astype(F32)
    mid = r.astype(BF16)
    lo = (r - mid.astype(F32)).astype(BF16)
    return hi, mid, lo


def _norm_matmul_body(x_ref, g_ref, w_ref, o_ref, *, n_chunk):
    xn = _rms(x_ref[...], g_ref[...]).astype(BF16)
    n = o_ref.shape[1]
    for c in range(0, n, n_chunk):
        cw = min(n_chunk, n - c)
        o_ref[:, c:c + cw] = _dot(xn, w_ref[:, c:c + cw]).astype(o_ref.dtype)


def norm_matmul(x, g, w, tm):
    m, d = x.shape
    n = w.shape[1]
    return pl.pallas_call(
        functools.partial(_norm_matmul_body, n_chunk=512),
        grid=(m // tm,),
        in_specs=[pl.BlockSpec((tm, d), lambda i: (i, 0)),
                  pl.BlockSpec((1, d), lambda i: (0, 0)),
                  pl.BlockSpec((d, n), lambda i: (0, 0))],
        out_specs=pl.BlockSpec((tm, n), lambda i: (i, 0)),
        out_shape=jax.ShapeDtypeStruct((m, n), BF16),
        compiler_params=_params(1),
        name="norm_matmul",
    )(x, g.reshape(1, d), w)


def _mm_norm_res_body(*refs, n_in):
    a_refs = refs[:n_in]
    w_refs = refs[n_in:2 * n_in]
    g_ref, h_ref, o_ref = refs[2 * n_in:]
    acc = _dot(a_refs[0][...], w_refs[0][...])
    for a_ref, w_ref in zip(a_refs[1:], w_refs[1:]):
        acc = acc + _dot(a_ref[...], w_ref[...])
    o_ref[...] = h_ref[...] + _rms(acc, g_ref[...])


def matmul_norm_res(a_list, w_list, g, h, tm):
    m, d = h.shape
    n_in = len(a_list)
    in_specs = [pl.BlockSpec((tm, a.shape[1]), lambda i: (i, 0)) for a in a_list]
    in_specs += [pl.BlockSpec(w.shape, lambda i: (0, 0)) for w in w_list]
    in_specs += [pl.BlockSpec((1, d), lambda i: (0, 0)),
                 pl.BlockSpec((tm, d), lambda i: (i, 0))]
    return pl.pallas_call(
        functools.partial(_mm_norm_res_body, n_in=n_in),
        grid=(m // tm,),
        in_specs=in_specs,
        out_specs=pl.BlockSpec((tm, d), lambda i: (i, 0)),
        out_shape=jax.ShapeDtypeStruct((m, d), F32),
        compiler_params=_params(1),
        name="matmul_norm_res",
    )(*a_list, *w_list, g.reshape(1, d), h)


def _xattn_body(h_ref, kv_ref, wq_ref, wo_ref, pg_ref, og_ref, o_ref):
    h = h_ref[...]
    hn = _rms(h, pg_ref[...]).astype(BF16)
    q = _dot(hn, wq_ref[...])
    scale = XA_DH ** -0.5
    outs = []
    for hd in range(XA_HEADS):
        lo = hd * XA_DH
        qh = q[:, lo:lo + XA_DH].astype(BF16)
        kh = kv_ref[:, lo:lo + XA_DH]
        vh = kv_ref[:, D_MODEL + lo:D_MODEL + lo + XA_DH]
        s = _dot_nt(qh, kh) * scale
        p = jnp.exp(s - jnp.max(s, axis=-1, keepdims=True))
        l = jnp.sum(p, axis=-1, keepdims=True)
        outs.append((_dot(p.astype(BF16), vh) / l).astype(BF16))
    o = jnp.concatenate(outs, axis=-1)
    o_ref[...] = h + _rms(_dot(o, wo_ref[...]), og_ref[...])


def xattn(h, kv, w_q, w_o, pre_g, post_g, seq_len, tm):
    m, d = h.shape
    blocks_per_seq = seq_len // tm
    return pl.pallas_call(
        _xattn_body,
        grid=(m // tm,),
        in_specs=[pl.BlockSpec((tm, d), lambda i: (i, 0)),
                  pl.BlockSpec((N_MEM, 2 * d), lambda i: (i // blocks_per_seq, 0)),
                  pl.BlockSpec((d, d), lambda i: (0, 0)),
                  pl.BlockSpec((d, d), lambda i: (0, 0)),
                  pl.BlockSpec((1, d), lambda i: (0, 0)),
                  pl.BlockSpec((1, d), lambda i: (0, 0))],
        out_specs=pl.BlockSpec((tm, d), lambda i: (i, 0)),
        out_shape=jax.ShapeDtypeStruct((m, d), F32),
        compiler_params=_params(1),
        name="xattn",
    )(h, kv, w_q, w_o, pre_g.reshape(1, d), post_g.reshape(1, d))


def _mlp_body(h_ref, w1_ref, w2_ref, pg_ref, og_ref, o_ref, *, ff_chunk):
    h = h_ref[...]
    hn = _rms(h, pg_ref[...]).astype(BF16)
    acc = jnp.zeros(h.shape, F32)
    for c in range(0, w1_ref.shape[1], ff_chunk):
        a = _dot(hn, w1_ref[:, c:c + ff_chunk])
        a = jnp.square(jnp.maximum(a, 0.0)).astype(BF16)
        acc = acc + _dot(a, w2_ref[c:c + ff_chunk, :])
    o_ref[...] = h + _rms(acc, og_ref[...])


def mlp(h, w1, w2, pre_g, post_g, tm):
    m, d = h.shape
    ff = w1.shape[1]
    return pl.pallas_call(
        functools.partial(_mlp_body, ff_chunk=512),
        grid=(m // tm,),
        in_specs=[pl.BlockSpec((tm, d), lambda i: (i, 0)),
                  pl.BlockSpec((d, ff), lambda i: (0, 0)),
                  pl.BlockSpec((ff, d), lambda i: (0, 0)),
                  pl.BlockSpec((1, d), lambda i: (0, 0)),
                  pl.BlockSpec((1, d), lambda i: (0, 0))],
        out_specs=pl.BlockSpec((tm, d), lambda i: (i, 0)),
        out_shape=jax.ShapeDtypeStruct((m, d), F32),
        compiler_params=_params(1),
        name="mlp",
    )(h, w1, w2, pre_g.reshape(1, d), post_g.reshape(1, d))


def _gla_constants():
    c = CHUNK
    t = np.arange(c)
    i = t[:, None]
    tt = t[None, :]
    mats = [(tt <= i), (tt > i)]
    masks = []
    for s in GLA_LEVELS:
        sb = i // (2 * s)
        r = sb * 2 * s + s - 1
        right = (i % (2 * s)) >= s
        d = np.where(right, (tt > r) & (tt <= i), (tt > i) & (tt <= r))
        mats.append(d)
        j = tt
        pair = right & ((j // (2 * s)) == sb) & ((j % (2 * s)) < s)
        masks.append(pair)
    masks.append(i == tt)
    dall = np.concatenate(mats, axis=0).astype(np.float32)
    lvl = np.stack([np.tile(m, (1, GLA_HEADS)) for m in masks]).astype(np.float32)
    hk = np.arange(GLA_KEY) // GLA_DK
    hv = np.arange(GLA_VAL) // GLA_DV
    bd_kk = (hk[:, None] == hk[None, :]).astype(np.float32)
    bd_kv = (hk[:, None] == hv[None, :]).astype(np.float32)
    bd_vk = bd_kv.T.copy()
    return (jnp.asarray(dall, BF16), jnp.asarray(lvl, F32), jnp.asarray(bd_kk, F32),
            jnp.asarray(bd_kv, BF16), jnp.asarray(bd_vk, F32))


def _gla_body(qk_ref, v_ref, go_ref, lr_ref, w2_ref, b_ref, ng_ref, dall_ref, lvl_ref,
              bdkk_ref, bdkv_ref, bdvk_ref, o_ref, st_ref, *, n_chunks):
    @pl.when(pl.program_id(1) == 0)
    def _():
        st_ref[...] = jnp.zeros_like(st_ref)

    dall = dall_ref[...]
    bdkk = bdkk_ref[...]
    bdkv = bdkv_ref[...]
    bdvk = bdvk_ref[...]
    n_lvl = len(GLA_LEVELS)

    def chunk(ci, carry):
        r0 = pl.multiple_of(ci * CHUNK, CHUNK)
        rows = pl.ds(r0, CHUNK)
        z = _dot(lr_ref[rows, :], w2_ref[...]) + b_ref[...]
        gk = -_softplus(-z) * (1.0 / GLA_GATE_NORM)
        g_hi, g_lo = _split2(gk)
        e_all = jnp.exp(jnp.minimum(_dot(dall, g_hi) + _dot(dall, g_lo), 0.0))
        q = qk_ref[rows, 0:GLA_KEY].astype(F32) * (GLA_DK ** -0.5)
        k = qk_ref[rows, GLA_KEY:2 * GLA_KEY].astype(F32)
        v = v_ref[rows, :]
        st = st_ref[...]

        o = _dot_nt((q * e_all[0:CHUNK]).astype(BF16), st.astype(BF16))
        k_up = (k * e_all[CHUNK:2 * CHUNK]).astype(BF16)
        st_ref[...] = st * e_all[CHUNK - 1:CHUNK] + bdvk * _dot_tn(v, k_up)

        a = jnp.zeros((CHUNK, GLA_KEY), F32)
        for li in range(n_lvl + 1):
            if li < n_lvl:
                e = e_all[(2 + li) * CHUNK:(3 + li) * CHUNK]
                qs, ks = q * e, k * e
            else:
                qs, ks = q, k
            kbd = (jnp.concatenate([ks] * GLA_HEADS, axis=0) * bdkk).astype(BF16)
            a = a + _dot_nt(qs.astype(BF16), kbd) * lvl_ref[li]
        vbd = jnp.concatenate([v] * GLA_HEADS, axis=0) * bdkv
        o = o + _dot(a.astype(BF16), vbd)

        ng = ng_ref[...]
        outs = []
        for hd in range(GLA_HEADS):
            sl = slice(hd * GLA_DV, (hd + 1) * GLA_DV)
            gate = _silu(go_ref[rows, sl].astype(F32))
            outs.append(_rms(o[:, sl], ng) * gate)
        o_ref[rows, :] = jnp.concatenate(outs, axis=-1).astype(o_ref.dtype)
        return carry

    lax.fori_loop(0, n_chunks, chunk, 0)


def gla(proj, gk_w2, gk_b, norm_g, batch, seq_len, tb):
    m = proj.shape[0]
    nb = seq_len // tb
    dall, lvl, bdkk, bdkv, bdvk = _gla_constants()
    w2 = jnp.zeros((LANE, GLA_KEY), F32).at[:GLA_GATE_RANK].set(gk_w2).astype(BF16)
    row = lambda b, t: (b * nb + t, 0)
    const2 = lambda b, t: (0, 0)
    return pl.pallas_call(
        functools.partial(_gla_body, n_chunks=tb // CHUNK),
        grid=(batch, nb),
        in_specs=[pl.BlockSpec((tb, 2 * GLA_KEY), lambda b, t: (b * nb + t, 0)),
                  pl.BlockSpec((tb, GLA_VAL), lambda b, t: (b * nb + t, 1)),
                  pl.BlockSpec((tb, GLA_VAL), lambda b, t: (b * nb + t, 2)),
                  pl.BlockSpec((tb, LANE), lambda b, t: (b * nb + t, 20)),
                  pl.BlockSpec((LANE, GLA_KEY), const2),
                  pl.BlockSpec((1, GLA_KEY), const2),
                  pl.BlockSpec((1, GLA_DV), const2),
                  pl.BlockSpec(dall.shape, const2),
                  pl.BlockSpec(lvl.shape, lambda b, t: (0, 0, 0)),
                  pl.BlockSpec(bdkk.shape, const2),
                  pl.BlockSpec(bdkv.shape, const2),
                  pl.BlockSpec(bdvk.shape, const2)],
        out_specs=pl.BlockSpec((tb, GLA_VAL), row),
        out_shape=jax.ShapeDtypeStruct((m, GLA_VAL), BF16),
        scratch_shapes=[pltpu.VMEM((GLA_VAL, GLA_KEY), F32)],
        compiler_params=_params(2),
        name="gla",
    )(proj, proj, proj, proj, w2, gk_b.reshape(1, GLA_KEY), norm_g.reshape(1, GLA_DV),
      dall, lvl, bdkk, bdkv, bdvk)


def _gelu_tanh(x):
    c = np.float32(np.sqrt(2.0 / np.pi))
    return 0.5 * x * (1.0 + jnp.tanh(c * (x + 0.044715 * (x * x * x))))


def _lru_body(x_ref, gate_ref, cw_ref, cb_ref, wa_ref, ba_ref, wx_ref, bx_ref, lam_ref,
              y_ref, xpad_ref, hc_ref, *, tb):
    @pl.when(pl.program_id(1) == 0)
    def _():
        xpad_ref[0:8, :] = jnp.zeros((8, LRU_WIDTH), F32)
        hc_ref[...] = jnp.zeros_like(hc_ref)

    x = x_ref[...].astype(F32)
    xpad_ref[8:8 + tb, :] = x
    xr = cb_ref[...] + cw_ref[0:1, :] * xpad_ref[5:5 + tb, :]
    for kk in range(1, CONV_W):
        xr = xr + cw_ref[kk:kk + 1, :] * xpad_ref[5 + kk:5 + kk + tb, :]
    xpad_ref[0:8, :] = x[tb - 8:tb, :]

    xb = xr.astype(BF16)
    ra, ia = [], []
    for n in range(LRU_BLOCKS):
        sl = slice(n * LRU_BLOCK, (n + 1) * LRU_BLOCK)
        ra.append(_dot(xb[:, sl], wa_ref[n]))
        ia.append(_dot(xb[:, sl], wx_ref[n]))
    r = _sigmoid(jnp.concatenate(ra, axis=-1) + ba_ref[...])
    ig = _sigmoid(jnp.concatenate(ia, axis=-1) + bx_ref[...])
    log_a = (-LRU_C) * r * _softplus(-lam_ref[...])
    a = jnp.exp(log_a)
    u = jnp.sqrt(1.0 - jnp.exp(2.0 * log_a)) * (ig * xr)

    rows = lax.broadcasted_iota(jnp.int32, (tb, LRU_WIDTH), 0)
    s = 1
    while s < tb:
        keep = rows >= s
        a_sh = jnp.where(keep, pltpu.roll(a, s, 0), 1.0)
        u_sh = jnp.where(keep, pltpu.roll(u, s, 0), 0.0)
        u = a * u_sh + u
        a = a * a_sh
        s *= 2
    h = a * hc_ref[7:8, :] + u
    hc_ref[...] = h[tb - 8:tb, :]
    y_ref[...] = (h * _gelu_tanh(gate_ref[...].astype(F32))).astype(y_ref.dtype)


def lru(proj, conv_w, conv_b, w_a, b_a, w_x, b_x, lam, batch, seq_len, tb):
    m = proj.shape[0]
    nb = seq_len // tb
    w = LRU_WIDTH
    const2 = lambda b, t: (0, 0)
    const3 = lambda b, t: (0, 0, 0)
    return pl.pallas_call(
        functools.partial(_lru_body, tb=tb),
        grid=(batch, nb),
        in_specs=[pl.BlockSpec((tb, w), lambda b, t: (b * nb + t, 3)),
                  pl.BlockSpec((tb, w), lambda b, t: (b * nb + t, 4)),
                  pl.BlockSpec((CONV_W, w), const2),
                  pl.BlockSpec((1, w), const2),
                  pl.BlockSpec((LRU_BLOCKS, LRU_BLOCK, LRU_BLOCK), const3),
                  pl.BlockSpec((1, w), const2),
                  pl.BlockSpec((LRU_BLOCKS, LRU_BLOCK, LRU_BLOCK), const3),
                  pl.BlockSpec((1, w), const2),
                  pl.BlockSpec((1, w), const2)],
        out_specs=pl.BlockSpec((tb, w), lambda b, t: (b * nb + t, 0)),
        out_shape=jax.ShapeDtypeStruct((m, w), BF16),
        scratch_shapes=[pltpu.VMEM((tb + 8, w), F32), pltpu.VMEM((8, w), F32)],
        compiler_params=_params(2),
        name="lru",
    )(proj, proj, conv_w, conv_b.reshape(1, w), w_a.astype(BF16), b_a.reshape(1, w),
      w_x.astype(BF16), b_x.reshape(1, w), lam.reshape(1, w))


def _gdn_constants():
    c = CHUNK
    t = np.arange(c)
    i = t[:, None]
    j = t[None, :]
    csum = np.concatenate([(j <= i), (j > i)], axis=0).astype(np.float32)
    tril_t = (i <= j).astype(np.float32)
    masks = [(i >= j), (i > j)]
    s = 1
    lv = []
    for s in (1,) + GDN_LEVELS:
        sb = i // (2 * s)
        pair = ((i % (2 * s)) >= s) & ((j // (2 * s)) == sb) & ((j % (2 * s)) < s)
        lv.append(pair)
    masks = np.stack(masks + lv).astype(np.float32)
    eye = np.eye(c, dtype=np.float32)
    return jnp.asarray(csum, BF16), jnp.asarray(tril_t, BF16), jnp.asarray(masks, F32), jnp.asarray(eye, F32)


def _gdn_body(q_ref, k_ref, v_ref, z_ref, bl_ref, al_ref, cw_ref, alog_ref, dtb_ref, ng_ref,
              csum_ref, trilt_ref, masks_ref, eye_ref, o_ref, xpad_ref, s_ref, *, tb):
    @pl.when(pl.program_id(1) == 0)
    def _():
        xpad_ref[0:8, :] = jnp.zeros((8, 3 * GDN_KEY), F32)
        s_ref[...] = jnp.zeros_like(s_ref)

    for part, ref in enumerate((q_ref, k_ref, v_ref)):
        xpad_ref[8:8 + tb, part * GDN_KEY:(part + 1) * GDN_KEY] = ref[...].astype(F32)
    y = cw_ref[0:1, :] * xpad_ref[5:5 + tb, :]
    for kk in range(1, CONV_W):
        y = y + cw_ref[kk:kk + 1, :] * xpad_ref[5 + kk:5 + kk + tb, :]
    xpad_ref[0:8, :] = xpad_ref[tb:tb + 8, :]
    y = _silu(y)

    beta_all = _sigmoid(bl_ref[...].astype(F32))
    g_all = -jnp.exp(alog_ref[...]) * _softplus(al_ref[...].astype(F32) + dtb_ref[...])
    csum = csum_ref[...]
    trilt = trilt_ref[...]
    incl = masks_ref[0]
    strict = masks_ref[1]
    eye = eye_ref[...]
    ng = ng_ref[...]
    scale = GDN_DK ** -0.5

    for ci in range(tb // CHUNK):
        r0 = ci * CHUNK
        g_c = g_all[r0:r0 + CHUNK, :]
        parts = _split3(g_c)
        gcol = sum(_dot(csum, p) for p in parts)
        grow = sum(_dot_tn(p, trilt) for p in parts)
        for hd in range(GDN_HEADS):
            sl = slice(hd * GDN_DK, (hd + 1) * GDN_DK)
            qh = y[r0:r0 + CHUNK, hd * GDN_DK:(hd + 1) * GDN_DK]
            kh = y[r0:r0 + CHUNK, GDN_KEY + hd * GDN_DK:GDN_KEY + (hd + 1) * GDN_DK]
            vh = y[r0:r0 + CHUNK, 2 * GDN_KEY + hd * GDN_DV:2 * GDN_KEY + (hd + 1) * GDN_DV]
            qh = qh * lax.rsqrt(jnp.sum(qh * qh, axis=-1, keepdims=True) + EPS) * scale
            kh = kh * lax.rsqrt(jnp.sum(kh * kh, axis=-1, keepdims=True) + EPS)
            beta = beta_all[r0:r0 + CHUNK, hd:hd + 1]
            gc = gcol[0:CHUNK, hd:hd + 1]
            gs = gcol[CHUNK:2 * CHUNK, hd:hd + 1]
            gr = grow[hd:hd + 1, :]
            decay = jnp.exp(jnp.minimum(gc - gr, 0.0)) * incl
            kb = kh * beta
            kb16 = kb.astype(BF16)
            k16 = kh.astype(BF16)
            m = _dot_nt(kb16, k16) * decay * strict
            aqk = _dot_nt(qh.astype(BF16), k16) * decay

            tinv = eye - m * masks_ref[2]
            for li in range(len(GDN_LEVELS)):
                n_s = (m * masks_ref[3 + li]).astype(BF16)
                p = _dot(tinv.astype(BF16), n_s)
                tinv = tinv - _dot(p.astype(BF16), tinv.astype(BF16))
            t16 = tinv.astype(BF16)
            u = _dot(t16, (vh * beta).astype(BF16))
            w = _dot(t16, (kb * jnp.exp(gc)).astype(BF16))

            s = s_ref[hd]
            s16 = s.astype(BF16)
            v_new = u - _dot(w.astype(BF16), s16)
            vn16 = v_new.astype(BF16)
            o = _dot((qh * jnp.exp(gc)).astype(BF16), s16) + _dot(aqk.astype(BF16), vn16)
            glast = gcol[CHUNK - 1:CHUNK, hd:hd + 1]
            s_ref[hd] = s * jnp.exp(glast) + _dot_tn((kh * jnp.exp(gs)).astype(BF16), vn16)

            gate = _silu(z_ref[r0:r0 + CHUNK, sl].astype(F32))
            o_ref[r0:r0 + CHUNK, sl] = (_rms(o, ng) * gate).astype(o_ref.dtype)


def gdn(proj, conv_w, a_log, dt_bias, norm_g, batch, seq_len, tb):
    m = proj.shape[0]
    nb = seq_len // tb
    csum, trilt, masks, eye = _gdn_constants()
    pad8 = lambda vec: jnp.zeros((1, LANE), F32).at[0, :GDN_HEADS].set(vec)
    const2 = lambda b, t: (0, 0)
    col = lambda c: (lambda b, t: (b * nb + t, c))
    nk = GDN_KEY // LANE
    return pl.pallas_call(
        functools.partial(_gdn_body, tb=tb),
        grid=(batch, nb),
        in_specs=[pl.BlockSpec((tb, GDN_KEY), col(0)),
                  pl.BlockSpec((tb, GDN_KEY), col(1)),
                  pl.BlockSpec((tb, GDN_VAL), col(2)),
                  pl.BlockSpec((tb, GDN_VAL), col(3)),
                  pl.BlockSpec((tb, LANE), col(4 * nk)),
                  pl.BlockSpec((tb, LANE), col(4 * nk + 1)),
                  pl.BlockSpec((CONV_W, 3 * GDN_KEY), const2),
                  pl.BlockSpec((1, LANE), const2),
                  pl.BlockSpec((1, LANE), const2),
                  pl.BlockSpec((1, GDN_DV), const2),
                  pl.BlockSpec(csum.shape, const2),
                  pl.BlockSpec(trilt.shape, const2),
                  pl.BlockSpec(masks.shape, lambda b, t: (0, 0, 0)),
                  pl.BlockSpec(eye.shape, const2)],
        out_specs=pl.BlockSpec((tb, GDN_VAL), col(0)),
        out_shape=jax.ShapeDtypeStruct((m, GDN_VAL), BF16),
        scratch_shapes=[pltpu.VMEM((tb + 8, 3 * GDN_KEY), F32),
                        pltpu.VMEM((GDN_HEADS, GDN_DK, GDN_DV), F32)],
        compiler_params=_params(2),
        name="gdn",
    )(proj, proj, proj, proj, proj, proj, conv_w, pad8(a_log), pad8(dt_bias),
      norm_g.reshape(1, GDN_DV), csum, trilt, masks, eye)


def _pad_cols(w, width):
    return jnp.pad(w, ((0, 0), (0, width - w.shape[1])))


def _even_w_in(w):
    o1 = GLA_KEY
    o2 = o1 + GLA_KEY
    o3 = o2 + GLA_VAL
    o4 = o3 + GLA_VAL
    o5 = o4 + GLA_GATE_RANK
    return jnp.concatenate([w[:, :o4], w[:, o5:], _pad_cols(w[:, o4:o5], LANE)], axis=1).astype(BF16)


def _odd_w_in(w):
    p2 = 3 * GDN_KEY + GDN_VAL
    p3 = p2 + GDN_HEADS
    return jnp.concatenate([w[:, :p2], _pad_cols(w[:, p2:p3], LANE), _pad_cols(w[:, p3:], LANE)],
                           axis=1).astype(BF16)


def even_mixer(h, pre_g, post_g, w_in, gk_w2, gk_b, gla_g, conv_w, conv_b, w_a, b_a, w_x, b_x, lam,
               w_out, batch, seq_len, tm, tb):
    proj = norm_matmul(h, pre_g, _even_w_in(w_in), tm)
    o = gla(proj, gk_w2, gk_b, gla_g, batch, seq_len, tb)
    y = lru(proj, conv_w, conv_b, w_a, b_a, w_x, b_x, lam, batch, seq_len, tb)
    wo = w_out.astype(BF16)
    return matmul_norm_res([o, y], [wo[:GLA_VAL], wo[GLA_VAL:]], post_g, h, tm)


def odd_mixer(h, pre_g, post_g, w_in, conv_w, a_log, dt_bias, norm_g, w_out, batch, seq_len, tm, tb):
    proj = norm_matmul(h, pre_g, _odd_w_in(w_in), tm)
    o = gdn(proj, conv_w, a_log, dt_bias, norm_g, batch, seq_len, tb)
    return matmul_norm_res([o], [w_out.astype(BF16)], post_g, h, tm)


def kernel(x, mem, mix_pre_g, mix_post_g, ab_w_in, gla_gk_w2, gla_gk_b, gla_norm_g, lru_conv_w, lru_conv_b, lru_w_a, lru_b_a, lru_w_x, lru_b_x, lru_lambda, ab_w_out, gdn_w_in, gdn_conv_w, gdn_a_log, gdn_dt_bias, gdn_norm_g, gdn_w_out, xa_pre_g, xa_mem_g, xa_post_g, xa_w_q, xa_w_kv, xa_w_o, mlp_pre_g, mlp_post_g, mlp_w1, mlp_w2):
    batch, seq_len, d = x.shape
    depth = mix_pre_g.shape[0]
    tm = min(512, seq_len)
    tb_even = min(256, seq_len)
    tb_odd = min(128, seq_len)
    h = x.reshape(batch * seq_len, d)
    mem2 = mem.reshape(batch * mem.shape[1], d)
    for i in range(depth):
        j = i // 2
        if i % 2 == 0:
            h = even_mixer(h, mix_pre_g[i], mix_post_g[i], ab_w_in[j], gla_gk_w2[j], gla_gk_b[j],
                           gla_norm_g[j], lru_conv_w[j], lru_conv_b[j], lru_w_a[j], lru_b_a[j],
                           lru_w_x[j], lru_b_x[j], lru_lambda[j], ab_w_out[j], batch, seq_len, tm, tb_even)
        else:
            h = odd_mixer(h, mix_pre_g[i], mix_post_g[i], gdn_w_in[j], gdn_conv_w[j], gdn_a_log[j],
                          gdn_dt_bias[j], gdn_norm_g[j], gdn_w_out[j], batch, seq_len, tm, tb_odd)
        kv = norm_matmul(mem2, xa_mem_g[i], xa_w_kv[i].astype(BF16), min(512, mem2.shape[0]))
        h = xattn(h, kv, xa_w_q[i].astype(BF16), xa_w_o[i].astype(BF16), xa_pre_g[i], xa_post_g[i],
                  seq_len, min(tm, 256))
        h = mlp(h, mlp_w1[i].astype(BF16), mlp_w2[i].astype(BF16), mlp_pre_g[i], mlp_post_g[i],
                min(tm, 256))
    return h.reshape(batch, seq_len, d)
```

```python
import functools

import numpy as np
import jax
import jax.numpy as jnp
from jax import lax
from jax.experimental import pallas as pl
from jax.experimental.pallas import tpu as pltpu

F32 = jnp.float32
BF16 = jnp.bfloat16

D_MODEL = 1024
N_MEM = 256
EPS = 1e-6
CHUNK = 64
CONV_W = 4
GLA_HEADS = 4
GLA_DK = 64
GLA_DV = 128
GLA_KEY = GLA_HEADS * GLA_DK
GLA_VAL = GLA_HEADS * GLA_DV
GLA_GATE_RANK = 16
GLA_GATE_NORM = 16.0
LRU_WIDTH = D_MODEL // 2
LRU_BLOCKS = 4
LRU_BLOCK = LRU_WIDTH // LRU_BLOCKS
LRU_C = 8.0
GDN_HEADS = 8
GDN_DK = 128
GDN_DV = 128
GDN_KEY = GDN_HEADS * GDN_DK
GDN_VAL = GDN_HEADS * GDN_DV
XA_HEADS = 4
XA_DH = D_MODEL // XA_HEADS
D_FF = 4 * D_MODEL

LANE = 128
VMEM_LIMIT = 56 * 1024 * 1024

GLA_LEVELS = (32, 16, 8, 4, 2, 1)
GDN_LEVELS = (2, 4, 8, 16, 32)

NT_DIMS = (((1,), (1,)), ((), ()))


def _params(n_axes):
    return pltpu.CompilerParams(
        dimension_semantics=("arbitrary",) * n_axes, vmem_limit_bytes=VMEM_LIMIT)


def _rms(x, g):
    ms = jnp.mean(x * x, axis=-1, keepdims=True)
    return x * lax.rsqrt(ms + EPS) * g


def _softplus(x):
    return jnp.maximum(x, 0.0) + jnp.log1p(jnp.exp(-jnp.abs(x)))


def _sigmoid(x):
    return 0.5 + 0.5 * jnp.tanh(0.5 * x)


def _silu(x):
    hx = 0.5 * x
    return hx + hx * jnp.tanh(hx)


def _causal_conv(xpad_ref, cw_ref, tb):
    xfull = xpad_ref[...]
    y = cw_ref[CONV_W - 1:CONV_W, :] * xfull[8:8 + tb]
    for kk in range(CONV_W - 1):
        y = y + cw_ref[kk:kk + 1, :] * pltpu.roll(xfull, CONV_W - 1 - kk, 0)[8:8 + tb]
    return y


def _dot(a, b):
    return jnp.dot(a, b, preferred_element_type=F32)


def _dot_nt(a, b):
    return lax.dot_general(a, b, NT_DIMS, preferred_element_type=F32)


def _split2(x):
    hi = x.astype(BF16)
    lo = (x - hi.astype(F32)).astype(BF16)
    return hi, lo


def _split3(x):
    hi = x.astype(BF16)
    r = x - hi.astype(F32)
    mid = r.astype(BF16)
    lo = (r - mid.astype(F32)).astype(BF16)
    return hi, mid, lo


def _norm_matmul_body(x_ref, g_ref, w_ref, o_ref, *, n_chunk):
    xn = _rms(x_ref[...], g_ref[...]).astype(BF16)
    n = o_ref.shape[1]
    for c in range(0, n, n_chunk):
        cw = min(n_chunk, n - c)
        o_ref[:, c:c + cw] = _dot(xn, w_ref[:, c:c + cw]).astype(o_ref.dtype)


def norm_matmul(x, g, w, tm):
    m, d = x.shape
    n = w.shape[1]
    return pl.pallas_call(
        functools.partial(_norm_matmul_body, n_chunk=512),
        grid=(m // tm,),
        in_specs=[pl.BlockSpec((tm, d), lambda i: (i, 0)),
                  pl.BlockSpec((1, d), lambda i: (0, 0)),
                  pl.BlockSpec((d, n), lambda i: (0, 0))],
        out_specs=pl.BlockSpec((tm, n), lambda i: (i, 0)),
        out_shape=jax.ShapeDtypeStruct((m, n), BF16),
        compiler_params=_params(1),
        name="norm_matmul",
    )(x, g.reshape(1, d), w)


def _mm_norm_res_body(*refs, n_in):
    a_refs = refs[:n_in]
    w_refs = refs[n_in:2 * n_in]
    g_ref, h_ref, o_ref = refs[2 * n_in:]
    acc = _dot(a_refs[0][...], w_refs[0][...])
    for a_ref, w_ref in zip(a_refs[1:], w_refs[1:]):
        acc = acc + _dot(a_ref[...], w_ref[...])
    o_ref[...] = h_ref[...] + _rms(acc, g_ref[...])


def matmul_norm_res(a_list, w_list, g, h, tm):
    m, d = h.shape
    n_in = len(a_list)
    in_specs = [pl.BlockSpec((tm, a.shape[1]), lambda i: (i, 0)) for a in a_list]
    in_specs += [pl.BlockSpec(w.shape, lambda i: (0, 0)) for w in w_list]
    in_specs += [pl.BlockSpec((1, d), lambda i: (0, 0)),
                 pl.BlockSpec((tm, d), lambda i: (i, 0))]
    return pl.pallas_call(
        functools.partial(_mm_norm_res_body, n_in=n_in),
        grid=(m // tm,),
        in_specs=in_specs,
        out_specs=pl.BlockSpec((tm, d), lambda i: (i, 0)),
        out_shape=jax.ShapeDtypeStruct((m, d), F32),
        compiler_params=_params(1),
        name="matmul_norm_res",
    )(*a_list, *w_list, g.reshape(1, d), h)


def _xattn_body(h_ref, kv_ref, wq_ref, wo_ref, pg_ref, og_ref, o_ref):
    h = h_ref[...]
    hn = _rms(h, pg_ref[...]).astype(BF16)
    q = _dot(hn, wq_ref[...])
    scale = XA_DH ** -0.5
    outs = []
    for hd in range(XA_HEADS):
        lo = hd * XA_DH
        qh = q[:, lo:lo + XA_DH].astype(BF16)
        kh = kv_ref[:, lo:lo + XA_DH]
        vh = kv_ref[:, D_MODEL + lo:D_MODEL + lo + XA_DH]
        s = _dot_nt(qh, kh) * scale
        p = jnp.exp(s - jnp.max(s, axis=-1, keepdims=True))
        l = jnp.sum(p, axis=-1, keepdims=True)
        outs.append((_dot(p.astype(BF16), vh) / l).astype(BF16))
    o = jnp.concatenate(outs, axis=-1)
    o_ref[...] = h + _rms(_dot(o, wo_ref[...]), og_ref[...])


def xattn(h, kv, w_q, w_o, pre_g, post_g, seq_len, tm):
    m, d = h.shape
    blocks_per_seq = seq_len // tm
    return pl.pallas_call(
        _xattn_body,
        grid=(m // tm,),
        in_specs=[pl.BlockSpec((tm, d), lambda i: (i, 0)),
                  pl.BlockSpec((N_MEM, 2 * d), lambda i: (i // blocks_per_seq, 0)),
                  pl.BlockSpec((d, d), lambda i: (0, 0)),
                  pl.BlockSpec((d, d), lambda i: (0, 0)),
                  pl.BlockSpec((1, d), lambda i: (0, 0)),
                  pl.BlockSpec((1, d), lambda i: (0, 0))],
        out_specs=pl.BlockSpec((tm, d), lambda i: (i, 0)),
        out_shape=jax.ShapeDtypeStruct((m, d), F32),
        compiler_params=_params(1),
        name="xattn",
    )(h, kv, w_q, w_o, pre_g.reshape(1, d), post_g.reshape(1, d))


def _mlp_body(h_ref, w1_ref, w2_ref, pg_ref, og_ref, o_ref, *, ff_chunk):
    h = h_ref[...]
    hn = _rms(h, pg_ref[...]).astype(BF16)
    acc = jnp.zeros(h.shape, F32)
    for c in range(0, w1_ref.shape[1], ff_chunk):
        a = _dot(hn, w1_ref[:, c:c + ff_chunk])
        a = jnp.square(jnp.maximum(a, 0.0)).astype(BF16)
        acc = acc + _dot(a, w2_ref[c:c + ff_chunk, :])
    o_ref[...] = h + _rms(acc, og_ref[...])


def mlp(h, w1, w2, pre_g, post_g, tm):
    m, d = h.shape
    ff = w1.shape[1]
    return pl.pallas_call(
        functools.partial(_mlp_body, ff_chunk=512),
        grid=(m // tm,),
        in_specs=[pl.BlockSpec((tm, d), lambda i: (i, 0)),
                  pl.BlockSpec((d, ff), lambda i: (0, 0)),
                  pl.BlockSpec((ff, d), lambda i: (0, 0)),
                  pl.BlockSpec((1, d), lambda i: (0, 0)),
                  pl.BlockSpec((1, d), lambda i: (0, 0))],
        out_specs=pl.BlockSpec((tm, d), lambda i: (i, 0)),
        out_shape=jax.ShapeDtypeStruct((m, d), F32),
        compiler_params=_params(1),
        name="mlp",
    )(h, w1, w2, pre_g.reshape(1, d), post_g.reshape(1, d))


def _gla_constants():
    c = CHUNK
    t = np.arange(c)
    i = t[:, None]
    tt = t[None, :]
    mats = [(tt <= i), (tt > i)]
    masks = []
    for s in GLA_LEVELS:
        sb = i // (2 * s)
        r = sb * 2 * s + s - 1
        right = (i % (2 * s)) >= s
        d = np.where(right, (tt > r) & (tt <= i), (tt > i) & (tt <= r))
        mats.append(d)
        j = tt
        pair = right & ((j // (2 * s)) == sb) & ((j % (2 * s)) < s)
        masks.append(pair)
    masks.append(i == tt)
    dall = np.concatenate(mats, axis=0).astype(np.float32)
    lvl = np.stack([np.tile(m, (1, GLA_HEADS)) for m in masks]).astype(np.float32)
    hk = np.arange(GLA_KEY) // GLA_DK
    hv = np.arange(GLA_VAL) // GLA_DV
    bd_kk = (hk[:, None] == hk[None, :]).astype(np.float32)
    bd_kv = (hk[:, None] == hv[None, :]).astype(np.float32)
    return (jnp.asarray(dall, BF16), jnp.asarray(lvl, F32), jnp.asarray(bd_kk, F32),
            jnp.asarray(bd_kv, BF16), jnp.asarray(bd_kv, F32))


def _gla_body(qk_ref, v_ref, go_ref, lr_ref, w2_ref, b_ref, ng_ref, dall_ref, lvl_ref,
              bdkk_ref, bdkv_ref, bdkvf_ref, o_ref, st_ref, *, n_chunks):
    @pl.when(pl.program_id(1) == 0)
    def _():
        st_ref[...] = jnp.zeros_like(st_ref)

    dall = dall_ref[...]
    bdkk = bdkk_ref[...]
    bdkv = bdkv_ref[...]
    bdkv_f = bdkvf_ref[...]
    n_lvl = len(GLA_LEVELS)

    def chunk(ci, carry):
        r0 = pl.multiple_of(ci * CHUNK, CHUNK)
        rows = pl.ds(r0, CHUNK)
        z = _dot(lr_ref[rows, :], w2_ref[...]) + b_ref[...]
        gk = -_softplus(-z) * (1.0 / GLA_GATE_NORM)
        g_hi, g_lo = _split2(gk)
        e_all = jnp.exp(jnp.minimum(_dot(dall, g_hi) + _dot(dall, g_lo), 0.0))
        q = qk_ref[rows, 0:GLA_KEY].astype(F32) * (GLA_DK ** -0.5)
        k = qk_ref[rows, GLA_KEY:2 * GLA_KEY].astype(F32)
        v = v_ref[rows, :]
        st = st_ref[...]

        o = _dot((q * e_all[0:CHUNK]).astype(BF16), st.astype(BF16))
        k_up_t = (k * e_all[CHUNK:2 * CHUNK]).T.astype(BF16)
        d_col = e_all[CHUNK - 8:CHUNK].T[:, 7:8]
        st_ref[...] = st * d_col + bdkv_f * _dot(k_up_t, v)

        a = jnp.zeros((CHUNK, GLA_KEY), F32)
        for li in range(n_lvl + 1):
            if li < n_lvl:
                e = e_all[(2 + li) * CHUNK:(3 + li) * CHUNK]
                qs, ks = q * e, k * e
            else:
                qs, ks = q, k
            kbd = (jnp.concatenate([ks] * GLA_HEADS, axis=0) * bdkk).astype(BF16)
            a = a + _dot_nt(qs.astype(BF16), kbd) * lvl_ref[li]
        vbd = jnp.concatenate([v] * GLA_HEADS, axis=0) * bdkv
        o = o + _dot(a.astype(BF16), vbd)

        ng = ng_ref[...]
        outs = []
        for hd in range(GLA_HEADS):
            sl = slice(hd * GLA_DV, (hd + 1) * GLA_DV)
            gate = _silu(go_ref[rows, sl].astype(F32))
            outs.append(_rms(o[:, sl], ng) * gate)
        o_ref[rows, :] = jnp.concatenate(outs, axis=-1).astype(o_ref.dtype)
        return carry

    lax.fori_loop(0, n_chunks, chunk, 0)


def gla(proj, gk_w2, gk_b, norm_g, batch, seq_len, tb):
    m = proj.shape[0]
    nb = seq_len // tb
    dall, lvl, bdkk, bdkv, bdkv_f = _gla_constants()
    w2 = jnp.zeros((LANE, GLA_KEY), F32).at[:GLA_GATE_RANK].set(gk_w2).astype(BF16)
    row = lambda b, t: (b * nb + t, 0)
    const2 = lambda b, t: (0, 0)
    return pl.pallas_call(
        functools.partial(_gla_body, n_chunks=tb // CHUNK),
        grid=(batch, nb),
        in_specs=[pl.BlockSpec((tb, 2 * GLA_KEY), lambda b, t: (b * nb + t, 0)),
                  pl.BlockSpec((tb, GLA_VAL), lambda b, t: (b * nb + t, 1)),
                  pl.BlockSpec((tb, GLA_VAL), lambda b, t: (b * nb + t, 2)),
                  pl.BlockSpec((tb, LANE), lambda b, t: (b * nb + t, 20)),
                  pl.BlockSpec((LANE, GLA_KEY), const2),
                  pl.BlockSpec((1, GLA_KEY), const2),
                  pl.BlockSpec((1, GLA_DV), const2),
                  pl.BlockSpec(dall.shape, const2),
                  pl.BlockSpec(lvl.shape, lambda b, t: (0, 0, 0)),
                  pl.BlockSpec(bdkk.shape, const2),
                  pl.BlockSpec(bdkv.shape, const2),
                  pl.BlockSpec(bdkv_f.shape, const2)],
        out_specs=pl.BlockSpec((tb, GLA_VAL), row),
        out_shape=jax.ShapeDtypeStruct((m, GLA_VAL), BF16),
        scratch_shapes=[pltpu.VMEM((GLA_KEY, GLA_VAL), F32)],
        compiler_params=_params(2),
        name="gla",
    )(proj, proj, proj, proj, w2, gk_b.reshape(1, GLA_KEY), norm_g.reshape(1, GLA_DV),
      dall, lvl, bdkk, bdkv, bdkv_f)


def _gelu_tanh(x):
    c = np.float32(np.sqrt(2.0 / np.pi))
    return 0.5 * x * (1.0 + jnp.tanh(c * (x + 0.044715 * (x * x * x))))


def _lru_body(x_ref, gate_ref, cw_ref, cb_ref, wa_ref, ba_ref, wx_ref, bx_ref, lam_ref,
              y_ref, xpad_ref, hc_ref, *, tb):
    @pl.when(pl.program_id(1) == 0)
    def _():
        xpad_ref[0:8, :] = jnp.zeros((8, LRU_WIDTH), F32)
        hc_ref[...] = jnp.zeros_like(hc_ref)

    x = x_ref[...].astype(F32)
    xpad_ref[8:8 + tb, :] = x
    xr = _causal_conv(xpad_ref, cw_ref, tb) + cb_ref[...]
    xpad_ref[0:8, :] = x[tb - 8:tb, :]

    xb = xr.astype(BF16)
    ra, ia = [], []
    for n in range(LRU_BLOCKS):
        sl = slice(n * LRU_BLOCK, (n + 1) * LRU_BLOCK)
        ra.append(_dot(xb[:, sl], wa_ref[n]))
        ia.append(_dot(xb[:, sl], wx_ref[n]))
    r = _sigmoid(jnp.concatenate(ra, axis=-1) + ba_ref[...])
    ig = _sigmoid(jnp.concatenate(ia, axis=-1) + bx_ref[...])
    log_a = (-LRU_C) * r * _softplus(-lam_ref[...])
    a = jnp.exp(log_a)
    u = jnp.sqrt(1.0 - jnp.exp(2.0 * log_a)) * (ig * xr)

    rows = lax.broadcasted_iota(jnp.int32, (tb, LRU_WIDTH), 0)
    s = 1
    while s < tb:
        keep = rows >= s
        a_sh = jnp.where(keep, pltpu.roll(a, s, 0), 1.0)
        u_sh = jnp.where(keep, pltpu.roll(u, s, 0), 0.0)
        u = a * u_sh + u
        a = a * a_sh
        s *= 2
    h = a * hc_ref[7:8, :] + u
    hc_ref[...] = h[tb - 8:tb, :]
    y_ref[...] = (h * _gelu_tanh(gate_ref[...].astype(F32))).astype(y_ref.dtype)


def lru(proj, conv_w, conv_b, w_a, b_a, w_x, b_x, lam, batch, seq_len, tb):
    m = proj.shape[0]
    nb = seq_len // tb
    w = LRU_WIDTH
    const2 = lambda b, t: (0, 0)
    const3 = lambda b, t: (0, 0, 0)
    return pl.pallas_call(
        functools.partial(_lru_body, tb=tb),
        grid=(batch, nb),
        in_specs=[pl.BlockSpec((tb, w), lambda b, t: (b * nb + t, 3)),
                  pl.BlockSpec((tb, w), lambda b, t: (b * nb + t, 4)),
                  pl.BlockSpec((CONV_W, w), const2),
                  pl.BlockSpec((1, w), const2),
                  pl.BlockSpec((LRU_BLOCKS, LRU_BLOCK, LRU_BLOCK), const3),
                  pl.BlockSpec((1, w), const2),
                  pl.BlockSpec((LRU_BLOCKS, LRU_BLOCK, LRU_BLOCK), const3),
                  pl.BlockSpec((1, w), const2),
                  pl.BlockSpec((1, w), const2)],
        out_specs=pl.BlockSpec((tb, w), lambda b, t: (b * nb + t, 0)),
        out_shape=jax.ShapeDtypeStruct((m, w), BF16),
        scratch_shapes=[pltpu.VMEM((tb + 8, w), F32), pltpu.VMEM((8, w), F32)],
        compiler_params=_params(2),
        name="lru",
    )(proj, proj, conv_w, conv_b.reshape(1, w), w_a.astype(BF16), b_a.reshape(1, w),
      w_x.astype(BF16), b_x.reshape(1, w), lam.reshape(1, w))


def _gdn_constants():
    c = CHUNK
    t = np.arange(c)
    i = t[:, None]
    j = t[None, :]
    csum = np.concatenate([(j <= i), (j > i)], axis=0).astype(np.float32)
    masks = [(i >= j), (i > j)]
    s = 1
    lv = []
    for s in (1,) + GDN_LEVELS:
        sb = i // (2 * s)
        pair = ((i % (2 * s)) >= s) & ((j // (2 * s)) == sb) & ((j % (2 * s)) < s)
        lv.append(pair)
    masks = np.stack(masks + lv).astype(np.float32)
    eye = np.eye(c, dtype=np.float32)
    return jnp.asarray(csum, BF16), jnp.asarray(masks, F32), jnp.asarray(eye, F32)


def _gdn_body(q_ref, k_ref, v_ref, z_ref, bl_ref, al_ref, cw_ref, alog_ref, dtb_ref, ng_ref,
              csum_ref, masks_ref, eye_ref, o_ref, xpad_ref, s_ref, *, tb):
    @pl.when(pl.program_id(1) == 0)
    def _():
        xpad_ref[0:8, :] = jnp.zeros((8, 3 * GDN_KEY), F32)
        s_ref[...] = jnp.zeros_like(s_ref)

    for part, ref in enumerate((q_ref, k_ref, v_ref)):
        xpad_ref[8:8 + tb, part * GDN_KEY:(part + 1) * GDN_KEY] = ref[...].astype(F32)
    y = _silu(_causal_conv(xpad_ref, cw_ref, tb))
    xpad_ref[0:8, :] = xpad_ref[tb:tb + 8, :]

    beta_all = _sigmoid(bl_ref[...].astype(F32))
    g_all = -jnp.exp(alog_ref[...]) * _softplus(al_ref[...].astype(F32) + dtb_ref[...])
    csum = csum_ref[...]
    incl = masks_ref[0]
    strict = masks_ref[1]
    eye = eye_ref[...]
    ng = ng_ref[...]
    scale = GDN_DK ** -0.5

    n_chunks = tb // CHUNK
    heads = range(GDN_HEADS)
    items = [(ci, hd) for ci in range(n_chunks) for hd in heads]

    gcol, grow = [], []
    for ci in range(n_chunks):
        parts = _split3(g_all[ci * CHUNK:(ci + 1) * CHUNK, :])
        gcol.append(sum(_dot(csum, p) for p in parts))
        grow.append(gcol[ci][0:CHUNK, :].T)

    def rows(ci):
        return slice(ci * CHUNK, (ci + 1) * CHUNK)

    def cols(part, hd):
        return slice(part * GDN_KEY + hd * GDN_DK, part * GDN_KEY + (hd + 1) * GDN_DK)

    qn, kn, vb16, kb, eg, decay, kgt16 = {}, {}, {}, {}, {}, {}, {}
    for it in items:
        ci, hd = it
        qh = y[rows(ci), cols(0, hd)]
        kh = y[rows(ci), cols(1, hd)]
        qn[it] = qh * (lax.rsqrt(jnp.sum(qh * qh, axis=-1, keepdims=True) + EPS) * scale)
        kn[it] = kh * lax.rsqrt(jnp.sum(kh * kh, axis=-1, keepdims=True) + EPS)
        beta = beta_all[rows(ci), hd:hd + 1]
        gc = gcol[ci][0:CHUNK, hd:hd + 1]
        gs = gcol[ci][CHUNK:2 * CHUNK, hd:hd + 1]
        eg[it] = jnp.exp(gc)
        decay[it] = jnp.exp(jnp.minimum(gc - grow[ci][hd:hd + 1, :], 0.0)) * incl
        kb[it] = kn[it] * beta
        vb16[it] = (y[rows(ci), cols(2, hd)] * beta).astype(BF16)
        kgt16[it] = (kn[it] * jnp.exp(gs)).T.astype(BF16)
    k16 = {it: kn[it].astype(BF16) for it in items}
    kk_qk = {it: _dot_nt(jnp.concatenate([kb[it], qn[it]], axis=0).astype(BF16), k16[it]) for it in items}
    m = {it: kk_qk[it][0:CHUNK] * (decay[it] * strict) for it in items}
    aqk16 = {it: (kk_qk[it][CHUNK:2 * CHUNK] * decay[it]).astype(BF16) for it in items}

    tinv = {it: eye - m[it] * masks_ref[2] for it in items}
    for li in range(len(GDN_LEVELS)):
        t16 = {it: tinv[it].astype(BF16) for it in items}
        p16 = {it: _dot(t16[it], (m[it] * masks_ref[3 + li]).astype(BF16)).astype(BF16) for it in items}
        tinv = {it: tinv[it] - _dot(p16[it], t16[it]) for it in items}
    uw = {it: _dot(tinv[it].astype(BF16),
                   jnp.concatenate([vb16[it], (kb[it] * eg[it]).astype(BF16)], axis=1)) for it in items}
    wq16 = {it: jnp.concatenate([uw[it][:, GDN_DV:], qn[it] * eg[it]], axis=0).astype(BF16) for it in items}

    s = {hd: s_ref[hd] for hd in heads}
    for ci in range(n_chunks):
        ws = {hd: _dot(wq16[ci, hd], s[hd].astype(BF16)) for hd in heads}
        vn16 = {hd: (uw[ci, hd][:, :GDN_DV] - ws[hd][0:CHUNK]).astype(BF16) for hd in heads}
        o = {hd: ws[hd][CHUNK:2 * CHUNK] + _dot(aqk16[ci, hd], vn16[hd]) for hd in heads}
        s = {hd: s[hd] * jnp.exp(gcol[ci][CHUNK - 1:CHUNK, hd:hd + 1]) + _dot(kgt16[ci, hd], vn16[hd])
             for hd in heads}
        for hd in heads:
            sl = slice(hd * GDN_DV, (hd + 1) * GDN_DV)
            gate = _silu(z_ref[rows(ci), sl].astype(F32))
            o_ref[rows(ci), sl] = (_rms(o[hd], ng) * gate).astype(o_ref.dtype)
    for hd in heads:
        s_ref[hd] = s[hd]


def gdn(proj, conv_w, a_log, dt_bias, norm_g, batch, seq_len, tb):
    m = proj.shape[0]
    nb = seq_len // tb
    csum, masks, eye = _gdn_constants()
    pad8 = lambda vec: jnp.zeros((1, LANE), F32).at[0, :GDN_HEADS].set(vec)
    const2 = lambda b, t: (0, 0)
    col = lambda c: (lambda b, t: (b * nb + t, c))
    nk = GDN_KEY // LANE
    return pl.pallas_call(
        functools.partial(_gdn_body, tb=tb),
        grid=(batch, nb),
        in_specs=[pl.BlockSpec((tb, GDN_KEY), col(0)),
                  pl.BlockSpec((tb, GDN_KEY), col(1)),
                  pl.BlockSpec((tb, GDN_VAL), col(2)),
                  pl.BlockSpec((tb, GDN_VAL), col(3)),
                  pl.BlockSpec((tb, LANE), col(4 * nk)),
                  pl.BlockSpec((tb, LANE), col(4 * nk + 1)),
                  pl.BlockSpec((CONV_W, 3 * GDN_KEY), const2),
                  pl.BlockSpec((1, LANE), const2),
                  pl.BlockSpec((1, LANE), const2),
                  pl.BlockSpec((1, GDN_DV), const2),
                  pl.BlockSpec(csum.shape, const2),
                  pl.BlockSpec(masks.shape, lambda b, t: (0, 0, 0)),
                  pl.BlockSpec(eye.shape, const2)],
        out_specs=pl.BlockSpec((tb, GDN_VAL), col(0)),
        out_shape=jax.ShapeDtypeStruct((m, GDN_VAL), BF16),
        scratch_shapes=[pltpu.VMEM((tb + 8, 3 * GDN_KEY), F32),
                        pltpu.VMEM((GDN_HEADS, GDN_DK, GDN_DV), F32)],
        compiler_params=_params(2),
        name="gdn",
    )(proj, proj, proj, proj, proj, proj, conv_w, pad8(a_log), pad8(dt_bias),
      norm_g.reshape(1, GDN_DV), csum, masks, eye)


def _pad_cols(w, width):
    return jnp.pad(w, ((0, 0), (0, width - w.shape[1])))


def _even_w_in(w):
    o1 = GLA_KEY
    o2 = o1 + GLA_KEY
    o3 = o2 + GLA_VAL
    o4 = o3 + GLA_VAL
    o5 = o4 + GLA_GATE_RANK
    return jnp.concatenate([w[:, :o4], w[:, o5:], _pad_cols(w[:, o4:o5], LANE)], axis=1).astype(BF16)


def _odd_w_in(w):
    p2 = 3 * GDN_KEY + GDN_VAL
    p3 = p2 + GDN_HEADS
    return jnp.concatenate([w[:, :p2], _pad_cols(w[:, p2:p3], LANE), _pad_cols(w[:, p3:], LANE)],
                           axis=1).astype(BF16)


def even_mixer(h, pre_g, post_g, w_in, gk_w2, gk_b, gla_g, conv_w, conv_b, w_a, b_a, w_x, b_x, lam,
               w_out, batch, seq_len, tm, tb):
    proj = norm_matmul(h, pre_g, _even_w_in(w_in), tm)
    o = gla(proj, gk_w2, gk_b, gla_g, batch, seq_len, tb)
    y = lru(proj, conv_w, conv_b, w_a, b_a, w_x, b_x, lam, batch, seq_len, tb)
    wo = w_out.astype(BF16)
    return matmul_norm_res([o, y], [wo[:GLA_VAL], wo[GLA_VAL:]], post_g, h, tm)


def odd_mixer(h, pre_g, post_g, w_in, conv_w, a_log, dt_bias, norm_g, w_out, batch, seq_len, tm, tb):
    proj = norm_matmul(h, pre_g, _odd_w_in(w_in), tm)
    o = gdn(proj, conv_w, a_log, dt_bias, norm_g, batch, seq_len, tb)
    return matmul_norm_res([o], [w_out.astype(BF16)], post_g, h, tm)


def kernel(x, mem, mix_pre_g, mix_post_g, ab_w_in, gla_gk_w2, gla_gk_b, gla_norm_g, lru_conv_w, lru_conv_b, lru_w_a, lru_b_a, lru_w_x, lru_b_x, lru_lambda, ab_w_out, gdn_w_in, gdn_conv_w, gdn_a_log, gdn_dt_bias, gdn_norm_g, gdn_w_out, xa_pre_g, xa_mem_g, xa_post_g, xa_w_q, xa_w_kv, xa_w_o, mlp_pre_g, mlp_post_g, mlp_w1, mlp_w2):
    batch, seq_len, d = x.shape
    depth = mix_pre_g.shape[0]
    tm = min(512, seq_len)
    tb_even = min(256, seq_len)
    tb_odd = min(128, seq_len)
    h = x.reshape(batch * seq_len, d)
    mem2 = mem.reshape(batch * mem.shape[1], d)
    for i in range(depth):
        j = i // 2
        if i % 2 == 0:
            h = even_mixer(h, mix_pre_g[i], mix_post_g[i], ab_w_in[j], gla_gk_w2[j], gla_gk_b[j],
                           gla_norm_g[j], lru_conv_w[j], lru_conv_b[j], lru_w_a[j], lru_b_a[j],
                           lru_w_x[j], lru_b_x[j], lru_lambda[j], ab_w_out[j], batch, seq_len, tm, tb_even)
        else:
            h = odd_mixer(h, mix_pre_g[i], mix_post_g[i], gdn_w_in[j], gdn_conv_w[j], gdn_a_log[j],
                          gdn_dt_bias[j], gdn_norm_g[j], gdn_w_out[j], batch, seq_len, tm, tb_odd)
        kv = norm_matmul(mem2, xa_mem_g[i], xa_w_kv[i].astype(BF16), min(512, mem2.shape[0]))
        h = xattn(h, kv, xa_w_q[i].astype(BF16), xa_w_o[i].astype(BF16), xa_pre_g[i], xa_post_g[i],
                  seq_len, min(tm, 256))
        h = mlp(h, mlp_w1[i].astype(BF16), mlp_w2[i].astype(BF16), mlp_pre_g[i], mlp_post_g[i],
                min(tm, 256))
    return h.reshape(batch, seq_len, d)
```

```python
import functools

import numpy as np
import jax
import jax.numpy as jnp
from jax import lax
from jax.experimental import pallas as pl
from jax.experimental.pallas import tpu as pltpu

F32 = jnp.float32
BF16 = jnp.bfloat16

D_MODEL = 1024
N_MEM = 256
EPS = 1e-6
CHUNK = 64
CONV_W = 4
GLA_HEADS = 4
GLA_DK = 64
GLA_DV = 128
GLA_KEY = GLA_HEADS * GLA_DK
GLA_VAL = GLA_HEADS * GLA_DV
GLA_GATE_RANK = 16
GLA_GATE_NORM = 16.0
LRU_WIDTH = D_MODEL // 2
LRU_BLOCKS = 4
LRU_BLOCK = LRU_WIDTH // LRU_BLOCKS
LRU_C = 8.0
GDN_HEADS = 8
GDN_DK = 128
GDN_DV = 128
GDN_KEY = GDN_HEADS * GDN_DK
GDN_VAL = GDN_HEADS * GDN_DV
XA_HEADS = 4
XA_DH = D_MODEL // XA_HEADS
D_FF = 4 * D_MODEL

LANE = 128
VMEM_LIMIT = 56 * 1024 * 1024

GLA_LEVELS = (32, 16, 8, 4, 2, 1)
GDN_LEVELS = (2, 4, 8, 16, 32)

NT_DIMS = (((1,), (1,)), ((), ()))


def _params(n_axes):
    return pltpu.CompilerParams(
        dimension_semantics=("arbitrary",) * n_axes, vmem_limit_bytes=VMEM_LIMIT)


def _rms(x, g):
    ms = jnp.mean(x * x, axis=-1, keepdims=True)
    return x * lax.rsqrt(ms + EPS) * g


def _softplus(x):
    return jnp.maximum(x, 0.0) + jnp.log1p(jnp.exp(-jnp.abs(x)))


def _sigmoid(x):
    return 0.5 + 0.5 * jnp.tanh(0.5 * x)


def _silu(x):
    hx = 0.5 * x
    return hx + hx * jnp.tanh(hx)


def _causal_conv(xpad_ref, cw_ref, tb):
    c = xpad_ref.shape[1]
    nt = tb // 8
    xt = xpad_ref[...].reshape(nt + 1, 8, c)
    sub = lax.broadcasted_iota(jnp.int32, (nt, 8, c), 1)
    y = cw_ref[CONV_W - 1:CONV_W, :] * xt[1:]
    for kk in range(CONV_W - 1):
        s = CONV_W - 1 - kk
        r = pltpu.roll(xt, s, 1)
        y = y + cw_ref[kk:kk + 1, :] * jnp.where(sub < s, r[:-1], r[1:])
    return y.reshape(tb, c)


def _dot(a, b):
    return jnp.dot(a, b, preferred_element_type=F32)


def _dot_nt(a, b):
    return lax.dot_general(a, b, NT_DIMS, preferred_element_type=F32)


def _split2(x):
    hi = x.astype(BF16)
    lo = (x - hi.astype(F32)).astype(BF16)
    return hi, lo


def _split3(x):
    hi = x.astype(BF16)
    r = x - hi.astype(F32)
    mid = r.astype(BF16)
    lo = (r - mid.astype(F32)).astype(BF16)
    return hi, mid, lo


def _norm_matmul_body(x_ref, g_ref, w_ref, o_ref, *, n_chunk):
    xn = _rms(x_ref[...], g_ref[...]).astype(BF16)
    n = o_ref.shape[1]
    for c in range(0, n, n_chunk):
        cw = min(n_chunk, n - c)
        o_ref[:, c:c + cw] = _dot(xn, w_ref[:, c:c + cw]).astype(o_ref.dtype)


def norm_matmul(x, g, w, tm):
    m, d = x.shape
    n = w.shape[1]
    return pl.pallas_call(
        functools.partial(_norm_matmul_body, n_chunk=512),
        grid=(m // tm,),
        in_specs=[pl.BlockSpec((tm, d), lambda i: (i, 0)),
                  pl.BlockSpec((1, d), lambda i: (0, 0)),
                  pl.BlockSpec((d, n), lambda i: (0, 0))],
        out_specs=pl.BlockSpec((tm, n), lambda i: (i, 0)),
        out_shape=jax.ShapeDtypeStruct((m, n), BF16),
        compiler_params=_params(1),
        name="norm_matmul",
    )(x, g.reshape(1, d), w)


def _mm_norm_res_body(*refs, n_in):
    a_refs = refs[:n_in]
    w_refs = refs[n_in:2 * n_in]
    g_ref, h_ref, o_ref = refs[2 * n_in:]
    acc = _dot(a_refs[0][...], w_refs[0][...])
    for a_ref, w_ref in zip(a_refs[1:], w_refs[1:]):
        acc = acc + _dot(a_ref[...], w_ref[...])
    o_ref[...] = h_ref[...] + _rms(acc, g_ref[...])


def matmul_norm_res(a_list, w_list, g, h, tm):
    m, d = h.shape
    n_in = len(a_list)
    in_specs = [pl.BlockSpec((tm, a.shape[1]), lambda i: (i, 0)) for a in a_list]
    in_specs += [pl.BlockSpec(w.shape, lambda i: (0, 0)) for w in w_list]
    in_specs += [pl.BlockSpec((1, d), lambda i: (0, 0)),
                 pl.BlockSpec((tm, d), lambda i: (i, 0))]
    return pl.pallas_call(
        functools.partial(_mm_norm_res_body, n_in=n_in),
        grid=(m // tm,),
        in_specs=in_specs,
        out_specs=pl.BlockSpec((tm, d), lambda i: (i, 0)),
        out_shape=jax.ShapeDtypeStruct((m, d), F32),
        compiler_params=_params(1),
        name="matmul_norm_res",
    )(*a_list, *w_list, g.reshape(1, d), h)


def _xattn_body(h_ref, kv_ref, wq_ref, wo_ref, pg_ref, og_ref, o_ref):
    h = h_ref[...]
    hn = _rms(h, pg_ref[...]).astype(BF16)
    q = _dot(hn, wq_ref[...])
    scale = XA_DH ** -0.5
    outs = []
    for hd in range(XA_HEADS):
        lo = hd * XA_DH
        qh = q[:, lo:lo + XA_DH].astype(BF16)
        kh = kv_ref[:, lo:lo + XA_DH]
        vh = kv_ref[:, D_MODEL + lo:D_MODEL + lo + XA_DH]
        s = _dot_nt(qh, kh) * scale
        p = jnp.exp(s - jnp.max(s, axis=-1, keepdims=True))
        l = jnp.sum(p, axis=-1, keepdims=True)
        outs.append((_dot(p.astype(BF16), vh) / l).astype(BF16))
    o = jnp.concatenate(outs, axis=-1)
    o_ref[...] = h + _rms(_dot(o, wo_ref[...]), og_ref[...])


def xattn(h, kv, w_q, w_o, pre_g, post_g, seq_len, tm):
    m, d = h.shape
    blocks_per_seq = seq_len // tm
    return pl.pallas_call(
        _xattn_body,
        grid=(m // tm,),
        in_specs=[pl.BlockSpec((tm, d), lambda i: (i, 0)),
                  pl.BlockSpec((N_MEM, 2 * d), lambda i: (i // blocks_per_seq, 0)),
                  pl.BlockSpec((d, d), lambda i: (0, 0)),
                  pl.BlockSpec((d, d), lambda i: (0, 0)),
                  pl.BlockSpec((1, d), lambda i: (0, 0)),
                  pl.BlockSpec((1, d), lambda i: (0, 0))],
        out_specs=pl.BlockSpec((tm, d), lambda i: (i, 0)),
        out_shape=jax.ShapeDtypeStruct((m, d), F32),
        compiler_params=_params(1),
        name="xattn",
    )(h, kv, w_q, w_o, pre_g.reshape(1, d), post_g.reshape(1, d))


def _mlp_body(h_ref, w1_ref, w2_ref, pg_ref, og_ref, o_ref, *, ff_chunk):
    h = h_ref[...]
    hn = _rms(h, pg_ref[...]).astype(BF16)
    acc = jnp.zeros(h.shape, F32)
    for c in range(0, w1_ref.shape[1], ff_chunk):
        a = _dot(hn, w1_ref[:, c:c + ff_chunk])
        a = jnp.square(jnp.maximum(a, 0.0)).astype(BF16)
        acc = acc + _dot(a, w2_ref[c:c + ff_chunk, :])
    o_ref[...] = h + _rms(acc, og_ref[...])


def mlp(h, w1, w2, pre_g, post_g, tm):
    m, d = h.shape
    ff = w1.shape[1]
    return pl.pallas_call(
        functools.partial(_mlp_body, ff_chunk=512),
        grid=(m // tm,),
        in_specs=[pl.BlockSpec((tm, d), lambda i: (i, 0)),
                  pl.BlockSpec((d, ff), lambda i: (0, 0), pipeline_mode=pl.Buffered(1)),
                  pl.BlockSpec((ff, d), lambda i: (0, 0), pipeline_mode=pl.Buffered(1)),
                  pl.BlockSpec((1, d), lambda i: (0, 0)),
                  pl.BlockSpec((1, d), lambda i: (0, 0))],
        out_specs=pl.BlockSpec((tm, d), lambda i: (i, 0)),
        out_shape=jax.ShapeDtypeStruct((m, d), F32),
        compiler_params=_params(1),
        name="mlp",
    )(h, w1, w2, pre_g.reshape(1, d), post_g.reshape(1, d))


def _gla_constants():
    c = CHUNK
    t = np.arange(c)
    i = t[:, None]
    tt = t[None, :]
    mats = [(tt <= i), (tt > i)]
    masks = []
    for s in GLA_LEVELS:
        sb = i // (2 * s)
        r = sb * 2 * s + s - 1
        right = (i % (2 * s)) >= s
        d = np.where(right, (tt > r) & (tt <= i), (tt > i) & (tt <= r))
        mats.append(d)
        j = tt
        pair = right & ((j // (2 * s)) == sb) & ((j % (2 * s)) < s)
        masks.append(pair)
    masks.append(i == tt)
    dall = np.concatenate(mats, axis=0).astype(np.float32)
    lvl = np.stack([np.tile(m, (1, GLA_HEADS)) for m in masks]).astype(np.float32)
    hk = np.arange(GLA_KEY) // GLA_DK
    hv = np.arange(GLA_VAL) // GLA_DV
    bd_kk = (hk[:, None] == hk[None, :]).astype(np.float32)
    bd_kv = (hk[:, None] == hv[None, :]).astype(np.float32)
    return (jnp.asarray(dall, BF16), jnp.asarray(lvl, F32), jnp.asarray(bd_kk, F32),
            jnp.asarray(bd_kv, BF16), jnp.asarray(bd_kv, F32))


def _gla_body(qk_ref, v_ref, go_ref, lr_ref, w2_ref, b_ref, ng_ref, dall_ref, lvl_ref,
              bdkk_ref, bdkv_ref, bdkvf_ref, o_ref, st_ref, *, n_chunks):
    @pl.when(pl.program_id(1) == 0)
    def _():
        st_ref[...] = jnp.zeros_like(st_ref)

    dall = dall_ref[...]
    bdkk = bdkk_ref[...]
    bdkv = bdkv_ref[...]
    bdkv_f = bdkvf_ref[...]
    n_lvl = len(GLA_LEVELS)

    chunks = range(n_chunks)

    def rows(ci):
        return slice(ci * CHUNK, (ci + 1) * CHUNK)

    z = _dot(lr_ref[...], w2_ref[...]) + b_ref[...]
    gk = -_softplus(-z) * (1.0 / GLA_GATE_NORM)
    g_hi, g_lo = _split2(gk)
    e_all = [jnp.exp(jnp.minimum(_dot(dall, g_hi[rows(ci)]) + _dot(dall, g_lo[rows(ci)]), 0.0))
             for ci in chunks]
    q = [qk_ref[rows(ci), 0:GLA_KEY].astype(F32) * (GLA_DK ** -0.5) for ci in chunks]
    k = [qk_ref[rows(ci), GLA_KEY:2 * GLA_KEY].astype(F32) for ci in chunks]
    v = [v_ref[rows(ci), :] for ci in chunks]

    q_in = [(q[ci] * e_all[ci][0:CHUNK]).astype(BF16) for ci in chunks]
    k_up_t = [(k[ci] * e_all[ci][CHUNK:2 * CHUNK]).T.astype(BF16) for ci in chunks]
    d_col = [e_all[ci][CHUNK - 8:CHUNK].T[:, 7:8] for ci in chunks]
    kv = [bdkv_f * _dot(k_up_t[ci], v[ci]) for ci in chunks]

    a = [jnp.zeros((CHUNK, GLA_KEY), F32) for _ in chunks]
    for li in range(n_lvl + 1):
        for ci in chunks:
            if li < n_lvl:
                e = e_all[ci][(2 + li) * CHUNK:(3 + li) * CHUNK]
                qs, ks = q[ci] * e, k[ci] * e
            else:
                qs, ks = q[ci], k[ci]
            kbd = (jnp.concatenate([ks] * GLA_HEADS, axis=0) * bdkk).astype(BF16)
            a[ci] = a[ci] + _dot_nt(qs.astype(BF16), kbd) * lvl_ref[li]
    o = [_dot(a[ci].astype(BF16), jnp.concatenate([v[ci]] * GLA_HEADS, axis=0) * bdkv) for ci in chunks]

    st = st_ref[...]
    for ci in chunks:
        o[ci] = o[ci] + _dot(q_in[ci], st.astype(BF16))
        st = st * d_col[ci] + kv[ci]
    st_ref[...] = st

    ng = ng_ref[...]
    o = jnp.concatenate(o, axis=0)
    outs = []
    for hd in range(GLA_HEADS):
        sl = slice(hd * GLA_DV, (hd + 1) * GLA_DV)
        outs.append(_rms(o[:, sl], ng) * _silu(go_ref[:, sl].astype(F32)))
    o_ref[...] = jnp.concatenate(outs, axis=-1).astype(o_ref.dtype)


def gla(proj, gk_w2, gk_b, norm_g, batch, seq_len, tb):
    m = proj.shape[0]
    nb = seq_len // tb
    dall, lvl, bdkk, bdkv, bdkv_f = _gla_constants()
    w2 = jnp.zeros((LANE, GLA_KEY), F32).at[:GLA_GATE_RANK].set(gk_w2).astype(BF16)
    row = lambda b, t: (b * nb + t, 0)
    const2 = lambda b, t: (0, 0)
    return pl.pallas_call(
        functools.partial(_gla_body, n_chunks=tb // CHUNK),
        grid=(batch, nb),
        in_specs=[pl.BlockSpec((tb, 2 * GLA_KEY), lambda b, t: (b * nb + t, 0)),
                  pl.BlockSpec((tb, GLA_VAL), lambda b, t: (b * nb + t, 1)),
                  pl.BlockSpec((tb, GLA_VAL), lambda b, t: (b * nb + t, 2)),
                  pl.BlockSpec((tb, LANE), lambda b, t: (b * nb + t, 20)),
                  pl.BlockSpec((LANE, GLA_KEY), const2),
                  pl.BlockSpec((1, GLA_KEY), const2),
                  pl.BlockSpec((1, GLA_DV), const2),
                  pl.BlockSpec(dall.shape, const2),
                  pl.BlockSpec(lvl.shape, lambda b, t: (0, 0, 0)),
                  pl.BlockSpec(bdkk.shape, const2),
                  pl.BlockSpec(bdkv.shape, const2),
                  pl.BlockSpec(bdkv_f.shape, const2)],
        out_specs=pl.BlockSpec((tb, GLA_VAL), row),
        out_shape=jax.ShapeDtypeStruct((m, GLA_VAL), BF16),
        scratch_shapes=[pltpu.VMEM((GLA_KEY, GLA_VAL), F32)],
        compiler_params=_params(2),
        name="gla",
    )(proj, proj, proj, proj, w2, gk_b.reshape(1, GLA_KEY), norm_g.reshape(1, GLA_DV),
      dall, lvl, bdkk, bdkv, bdkv_f)


def _gelu_tanh(x):
    c = np.float32(np.sqrt(2.0 / np.pi))
    return 0.5 * x * (1.0 + jnp.tanh(c * (x + 0.044715 * (x * x * x))))


def _lru_body(x_ref, gate_ref, cw_ref, cb_ref, wa_ref, ba_ref, wx_ref, bx_ref, lam_ref,
              y_ref, xpad_ref, hc_ref, *, tb):
    @pl.when(pl.program_id(1) == 0)
    def _():
        xpad_ref[0:8, :] = jnp.zeros((8, LRU_WIDTH), F32)
        hc_ref[...] = jnp.zeros_like(hc_ref)

    x = x_ref[...].astype(F32)
    xpad_ref[8:8 + tb, :] = x
    xr = _causal_conv(xpad_ref, cw_ref, tb) + cb_ref[...]
    xpad_ref[0:8, :] = x[tb - 8:tb, :]

    xb = xr.astype(BF16)
    ra, ia = [], []
    for n in range(LRU_BLOCKS):
        sl = slice(n * LRU_BLOCK, (n + 1) * LRU_BLOCK)
        ra.append(_dot(xb[:, sl], wa_ref[n]))
        ia.append(_dot(xb[:, sl], wx_ref[n]))
    r = _sigmoid(jnp.concatenate(ra, axis=-1) + ba_ref[...])
    ig = _sigmoid(jnp.concatenate(ia, axis=-1) + bx_ref[...])
    log_a = (-LRU_C) * r * _softplus(-lam_ref[...])
    a = jnp.exp(log_a)
    u = jnp.sqrt(1.0 - jnp.exp(2.0 * log_a)) * (ig * xr)

    rows = lax.broadcasted_iota(jnp.int32, (tb, LRU_WIDTH), 0)
    s = 1
    while s < tb:
        keep = rows >= s
        a_sh = jnp.where(keep, pltpu.roll(a, s, 0), 1.0)
        u_sh = jnp.where(keep, pltpu.roll(u, s, 0), 0.0)
        u = a * u_sh + u
        a = a * a_sh
        s *= 2
    h = a * hc_ref[7:8, :] + u
    hc_ref[...] = h[tb - 8:tb, :]
    y_ref[...] = (h * _gelu_tanh(gate_ref[...].astype(F32))).astype(y_ref.dtype)


def lru(proj, conv_w, conv_b, w_a, b_a, w_x, b_x, lam, batch, seq_len, tb):
    m = proj.shape[0]
    nb = seq_len // tb
    w = LRU_WIDTH
    const2 = lambda b, t: (0, 0)
    const3 = lambda b, t: (0, 0, 0)
    return pl.pallas_call(
        functools.partial(_lru_body, tb=tb),
        grid=(batch, nb),
        in_specs=[pl.BlockSpec((tb, w), lambda b, t: (b * nb + t, 3)),
                  pl.BlockSpec((tb, w), lambda b, t: (b * nb + t, 4)),
                  pl.BlockSpec((CONV_W, w), const2),
                  pl.BlockSpec((1, w), const2),
                  pl.BlockSpec((LRU_BLOCKS, LRU_BLOCK, LRU_BLOCK), const3),
                  pl.BlockSpec((1, w), const2),
                  pl.BlockSpec((LRU_BLOCKS, LRU_BLOCK, LRU_BLOCK), const3),
                  pl.BlockSpec((1, w), const2),
                  pl.BlockSpec((1, w), const2)],
        out_specs=pl.BlockSpec((tb, w), lambda b, t: (b * nb + t, 0)),
        out_shape=jax.ShapeDtypeStruct((m, w), BF16),
        scratch_shapes=[pltpu.VMEM((tb + 8, w), F32), pltpu.VMEM((8, w), F32)],
        compiler_params=_params(2),
        name="lru",
    )(proj, proj, conv_w, conv_b.reshape(1, w), w_a.astype(BF16), b_a.reshape(1, w),
      w_x.astype(BF16), b_x.reshape(1, w), lam.reshape(1, w))


def _gdn_constants():
    c = CHUNK
    t = np.arange(c)
    i = t[:, None]
    j = t[None, :]
    csum = np.concatenate([(j <= i), (j > i)], axis=0).astype(np.float32)
    masks = [(i >= j), (i > j)]
    s = 1
    lv = []
    for s in (1,) + GDN_LEVELS:
        sb = i // (2 * s)
        pair = ((i % (2 * s)) >= s) & ((j // (2 * s)) == sb) & ((j % (2 * s)) < s)
        lv.append(pair)
    masks = np.stack(masks + lv).astype(np.float32)
    eye = np.eye(c, dtype=np.float32)
    return jnp.asarray(csum, BF16), jnp.asarray(masks, F32), jnp.asarray(eye, F32)


def _gdn_body(q_ref, k_ref, v_ref, z_ref, bl_ref, al_ref, cw_ref, alog_ref, dtb_ref, ng_ref,
              csum_ref, masks_ref, eye_ref, o_ref, xpad_ref, s_ref, *, tb):
    @pl.when(pl.program_id(1) == 0)
    def _():
        xpad_ref[0:8, :] = jnp.zeros((8, 3 * GDN_KEY), F32)
        s_ref[...] = jnp.zeros_like(s_ref)

    for part, ref in enumerate((q_ref, k_ref, v_ref)):
        xpad_ref[8:8 + tb, part * GDN_KEY:(part + 1) * GDN_KEY] = ref[...].astype(F32)
    y = _silu(_causal_conv(xpad_ref, cw_ref, tb))
    xpad_ref[0:8, :] = xpad_ref[tb:tb + 8, :]

    beta_all = _sigmoid(bl_ref[...].astype(F32))
    g_all = -jnp.exp(alog_ref[...]) * _softplus(al_ref[...].astype(F32) + dtb_ref[...])
    csum = csum_ref[...]
    incl = masks_ref[0]
    strict = masks_ref[1]
    eye = eye_ref[...]
    ng = ng_ref[...]
    scale = GDN_DK ** -0.5

    n_chunks = tb // CHUNK
    heads = range(GDN_HEADS)
    items = [(ci, hd) for ci in range(n_chunks) for hd in heads]

    gcol, grow = [], []
    for ci in range(n_chunks):
        parts = _split3(g_all[ci * CHUNK:(ci + 1) * CHUNK, :])
        gcol.append(sum(_dot(csum, p) for p in parts))
        grow.append(gcol[ci][0:CHUNK, :].T)

    def rows(ci):
        return slice(ci * CHUNK, (ci + 1) * CHUNK)

    def cols(part, hd):
        return slice(part * GDN_KEY + hd * GDN_DK, part * GDN_KEY + (hd + 1) * GDN_DK)

    qn, kn, vb16, kb, eg, decay, kgt16 = {}, {}, {}, {}, {}, {}, {}
    for it in items:
        ci, hd = it
        qh = y[rows(ci), cols(0, hd)]
        kh = y[rows(ci), cols(1, hd)]
        qn[it] = qh * (lax.rsqrt(jnp.sum(qh * qh, axis=-1, keepdims=True) + EPS) * scale)
        kn[it] = kh * lax.rsqrt(jnp.sum(kh * kh, axis=-1, keepdims=True) + EPS)
        beta = beta_all[rows(ci), hd:hd + 1]
        gc = gcol[ci][0:CHUNK, hd:hd + 1]
        gs = gcol[ci][CHUNK:2 * CHUNK, hd:hd + 1]
        eg[it] = jnp.exp(gc)
        decay[it] = jnp.exp(jnp.minimum(gc - grow[ci][hd:hd + 1, :], 0.0)) * incl
        kb[it] = kn[it] * beta
        vb16[it] = (y[rows(ci), cols(2, hd)] * beta).astype(BF16)
        kgt16[it] = (kn[it] * jnp.exp(gs)).T.astype(BF16)
    k16 = {it: kn[it].astype(BF16) for it in items}
    kk_qk = {it: _dot_nt(jnp.concatenate([kb[it], qn[it]], axis=0).astype(BF16), k16[it]) for it in items}
    m = {it: kk_qk[it][0:CHUNK] * (decay[it] * strict) for it in items}
    aqk16 = {it: (kk_qk[it][CHUNK:2 * CHUNK] * decay[it]).astype(BF16) for it in items}

    tinv = {it: eye - m[it] * masks_ref[2] for it in items}
    for li in range(len(GDN_LEVELS)):
        t16 = {it: tinv[it].astype(BF16) for it in items}
        p16 = {it: _dot(t16[it], (m[it] * masks_ref[3 + li]).astype(BF16)).astype(BF16) for it in items}
        tinv = {it: tinv[it] - _dot(p16[it], t16[it]) for it in items}
    uw = {it: _dot(tinv[it].astype(BF16),
                   jnp.concatenate([vb16[it], (kb[it] * eg[it]).astype(BF16)], axis=1)) for it in items}
    wq16 = {it: jnp.concatenate([uw[it][:, GDN_DV:], qn[it] * eg[it]], axis=0).astype(BF16) for it in items}

    s = {hd: s_ref[hd] for hd in heads}
    for ci in range(n_chunks):
        ws = {hd: _dot(wq16[ci, hd], s[hd].astype(BF16)) for hd in heads}
        vn16 = {hd: (uw[ci, hd][:, :GDN_DV] - ws[hd][0:CHUNK]).astype(BF16) for hd in heads}
        o = {hd: ws[hd][CHUNK:2 * CHUNK] + _dot(aqk16[ci, hd], vn16[hd]) for hd in heads}
        s = {hd: s[hd] * jnp.exp(gcol[ci][CHUNK - 1:CHUNK, hd:hd + 1]) + _dot(kgt16[ci, hd], vn16[hd])
             for hd in heads}
        for hd in heads:
            sl = slice(hd * GDN_DV, (hd + 1) * GDN_DV)
            gate = _silu(z_ref[rows(ci), sl].astype(F32))
            o_ref[rows(ci), sl] = (_rms(o[hd], ng) * gate).astype(o_ref.dtype)
    for hd in heads:
        s_ref[hd] = s[hd]


def gdn(proj, conv_w, a_log, dt_bias, norm_g, batch, seq_len, tb):
    m = proj.shape[0]
    nb = seq_len // tb
    csum, masks, eye = _gdn_constants()
    pad8 = lambda vec: jnp.zeros((1, LANE), F32).at[0, :GDN_HEADS].set(vec)
    const2 = lambda b, t: (0, 0)
    col = lambda c: (lambda b, t: (b * nb + t, c))
    nk = GDN_KEY // LANE
    return pl.pallas_call(
        functools.partial(_gdn_body, tb=tb),
        grid=(batch, nb),
        in_specs=[pl.BlockSpec((tb, GDN_KEY), col(0)),
                  pl.BlockSpec((tb, GDN_KEY), col(1)),
                  pl.BlockSpec((tb, GDN_VAL), col(2)),
                  pl.BlockSpec((tb, GDN_VAL), col(3)),
                  pl.BlockSpec((tb, LANE), col(4 * nk)),
                  pl.BlockSpec((tb, LANE), col(4 * nk + 1)),
                  pl.BlockSpec((CONV_W, 3 * GDN_KEY), const2),
                  pl.BlockSpec((1, LANE), const2),
                  pl.BlockSpec((1, LANE), const2),
                  pl.BlockSpec((1, GDN_DV), const2),
                  pl.BlockSpec(csum.shape, const2),
                  pl.BlockSpec(masks.shape, lambda b, t: (0, 0, 0)),
                  pl.BlockSpec(eye.shape, const2)],
        out_specs=pl.BlockSpec((tb, GDN_VAL), col(0)),
        out_shape=jax.ShapeDtypeStruct((m, GDN_VAL), BF16),
        scratch_shapes=[pltpu.VMEM((tb + 8, 3 * GDN_KEY), F32),
                        pltpu.VMEM((GDN_HEADS, GDN_DK, GDN_DV), F32)],
        compiler_params=_params(2),
        name="gdn",
    )(proj, proj, proj, proj, proj, proj, conv_w, pad8(a_log), pad8(dt_bias),
      norm_g.reshape(1, GDN_DV), csum, masks, eye)


def _pad_cols(w, width):
    return jnp.pad(w, ((0, 0), (0, width - w.shape[1])))


def _even_w_in(w):
    o1 = GLA_KEY
    o2 = o1 + GLA_KEY
    o3 = o2 + GLA_VAL
    o4 = o3 + GLA_VAL
    o5 = o4 + GLA_GATE_RANK
    return jnp.concatenate([w[:, :o4], w[:, o5:], _pad_cols(w[:, o4:o5], LANE)], axis=1).astype(BF16)


def _odd_w_in(w):
    p2 = 3 * GDN_KEY + GDN_VAL
    p3 = p2 + GDN_HEADS
    return jnp.concatenate([w[:, :p2], _pad_cols(w[:, p2:p3], LANE), _pad_cols(w[:, p3:], LANE)],
                           axis=1).astype(BF16)


def even_mixer(h, pre_g, post_g, w_in, gk_w2, gk_b, gla_g, conv_w, conv_b, w_a, b_a, w_x, b_x, lam,
               w_out, batch, seq_len, tm, tb):
    proj = norm_matmul(h, pre_g, _even_w_in(w_in), tm)
    o = gla(proj, gk_w2, gk_b, gla_g, batch, seq_len, tb)
    y = lru(proj, conv_w, conv_b, w_a, b_a, w_x, b_x, lam, batch, seq_len, tb)
    wo = w_out.astype(BF16)
    return matmul_norm_res([o, y], [wo[:GLA_VAL], wo[GLA_VAL:]], post_g, h, tm)


def odd_mixer(h, pre_g, post_g, w_in, conv_w, a_log, dt_bias, norm_g, w_out, batch, seq_len, tm, tb):
    proj = norm_matmul(h, pre_g, _odd_w_in(w_in), tm)
    o = gdn(proj, conv_w, a_log, dt_bias, norm_g, batch, seq_len, tb)
    return matmul_norm_res([o], [w_out.astype(BF16)], post_g, h, tm)


def kernel(x, mem, mix_pre_g, mix_post_g, ab_w_in, gla_gk_w2, gla_gk_b, gla_norm_g, lru_conv_w, lru_conv_b, lru_w_a, lru_b_a, lru_w_x, lru_b_x, lru_lambda, ab_w_out, gdn_w_in, gdn_conv_w, gdn_a_log, gdn_dt_bias, gdn_norm_g, gdn_w_out, xa_pre_g, xa_mem_g, xa_post_g, xa_w_q, xa_w_kv, xa_w_o, mlp_pre_g, mlp_post_g, mlp_w1, mlp_w2):
    batch, seq_len, d = x.shape
    depth = mix_pre_g.shape[0]
    tm = min(512, seq_len)
    tb_even = min(256, seq_len)
    tb_odd = min(128, seq_len)
    h = x.reshape(batch * seq_len, d)
    mem2 = mem.reshape(batch * mem.shape[1], d)
    for i in range(depth):
        j = i // 2
        if i % 2 == 0:
            h = even_mixer(h, mix_pre_g[i], mix_post_g[i], ab_w_in[j], gla_gk_w2[j], gla_gk_b[j],
                           gla_norm_g[j], lru_conv_w[j], lru_conv_b[j], lru_w_a[j], lru_b_a[j],
                           lru_w_x[j], lru_b_x[j], lru_lambda[j], ab_w_out[j], batch, seq_len, tm, tb_even)
        else:
            h = odd_mixer(h, mix_pre_g[i], mix_post_g[i], gdn_w_in[j], gdn_conv_w[j], gdn_a_log[j],
                          gdn_dt_bias[j], gdn_norm_g[j], gdn_w_out[j], batch, seq_len, tm, tb_odd)
        kv = norm_matmul(mem2, xa_mem_g[i], xa_w_kv[i].astype(BF16), min(512, mem2.shape[0]))
        h = xattn(h, kv, xa_w_q[i].astype(BF16), xa_w_o[i].astype(BF16), xa_pre_g[i], xa_post_g[i],
                  seq_len, tm)
        h = mlp(h, mlp_w1[i].astype(BF16), mlp_w2[i].astype(BF16), mlp_pre_g[i], mlp_post_g[i], tm)
    return h.reshape(batch, seq_len, d)
```

```python
import functools

import numpy as np
import jax
import jax.numpy as jnp
from jax import lax
from jax.experimental import pallas as pl
from jax.experimental.pallas import tpu as pltpu

F32 = jnp.float32
BF16 = jnp.bfloat16

D_MODEL = 1024
N_MEM = 256
EPS = 1e-6
CHUNK = 64
CONV_W = 4
GLA_HEADS = 4
GLA_DK = 64
GLA_DV = 128
GLA_KEY = GLA_HEADS * GLA_DK
GLA_VAL = GLA_HEADS * GLA_DV
GLA_GATE_RANK = 16
GLA_GATE_NORM = 16.0
LRU_WIDTH = D_MODEL // 2
LRU_BLOCKS = 4
LRU_BLOCK = LRU_WIDTH // LRU_BLOCKS
LRU_C = 8.0
GDN_HEADS = 8
GDN_DK = 128
GDN_DV = 128
GDN_KEY = GDN_HEADS * GDN_DK
GDN_VAL = GDN_HEADS * GDN_DV
XA_HEADS = 4
XA_DH = D_MODEL // XA_HEADS
D_FF = 4 * D_MODEL

LANE = 128
VMEM_LIMIT = 56 * 1024 * 1024

GLA_LEVELS = (32, 16, 8, 4, 2, 1)
GDN_LEVELS = (2, 4, 8, 16, 32)

NT_DIMS = (((1,), (1,)), ((), ()))


def _params(n_axes):
    return pltpu.CompilerParams(
        dimension_semantics=("arbitrary",) * n_axes, vmem_limit_bytes=VMEM_LIMIT)


def _rms(x, g):
    ms = jnp.mean(x * x, axis=-1, keepdims=True)
    return x * lax.rsqrt(ms + EPS) * g


def _softplus(x):
    return jnp.maximum(x, 0.0) + jnp.log1p(jnp.exp(-jnp.abs(x)))


def _sigmoid(x):
    return 0.5 + 0.5 * jnp.tanh(0.5 * x)


def _silu(x):
    hx = 0.5 * x
    return hx + hx * jnp.tanh(hx)


def _dot(a, b):
    return jnp.dot(a, b, preferred_element_type=F32)


def _dot_nt(a, b):
    return lax.dot_general(a, b, NT_DIMS, preferred_element_type=F32)


def _split2(x):
    hi = x.astype(BF16)
    lo = (x - hi.astype(F32)).astype(BF16)
    return hi, lo


def _split3(x):
    hi = x.astype(BF16)
    r = x - hi.astype(F32)
    mid = r.astype(BF16)
    lo = (r - mid.astype(F32)).astype(BF16)
    return hi, mid, lo


def _shifted_taps(prev_tail, cur, cw):
    tm, c = cur.shape
    nt = tm // 8
    xt = jnp.concatenate([prev_tail, cur], axis=0).reshape(nt + 1, 8, c)
    sub = lax.broadcasted_iota(jnp.int32, (nt, 8, c), 1)
    y = cw[CONV_W - 1:CONV_W, :] * xt[1:]
    for kk in range(CONV_W - 1):
        s = CONV_W - 1 - kk
        r = pltpu.roll(xt, s, 1)
        y = y + cw[kk:kk + 1, :] * jnp.where(sub < s, r[:-1], r[1:])
    return y.reshape(tm, c)


def _norm_matmul_body(x_ref, g_ref, w_ref, *rest, n_chunk, conv, blocks_per_seq):
    if conv is None:
        (o_ref,) = rest
    else:
        cw_ref, cb_ref, o_ref, tail_ref = rest
        c0, c1, use_silu = conv

        @pl.when(pl.program_id(0) % blocks_per_seq == 0)
        def _():
            tail_ref[...] = jnp.zeros_like(tail_ref)

    tm = x_ref.shape[0]
    xn = _rms(x_ref[...], g_ref[...]).astype(BF16)
    n = o_ref.shape[1]
    starts = list(range(0, n, n_chunk))
    width = lambda c: min(n_chunk, n - c)
    res = _dot(xn, w_ref[:, 0:width(0)])
    for idx, c in enumerate(starts):
        cw = width(c)
        nxt = None
        if idx + 1 < len(starts):
            c2 = starts[idx + 1]
            nxt = _dot(xn, w_ref[:, c2:c2 + width(c2)])
        if conv is not None and c0 <= c < c1:
            lc = slice(c - c0, c - c0 + cw)
            y = _shifted_taps(tail_ref[:, lc], res, cw_ref[:, lc]) + cb_ref[:, lc]
            tail_ref[:, lc] = res[tm - 8:tm, :]
            res = _silu(y) if use_silu else y
        o_ref[:, c:c + cw] = res.astype(o_ref.dtype)
        res = nxt


def norm_matmul(x, g, w, tm, conv=None, conv_w=None, conv_b=None, seq_len=None):
    m, d = x.shape
    n = w.shape[1]
    const = lambda i: (0, 0)
    in_specs = [pl.BlockSpec((tm, d), lambda i: (i, 0)),
                pl.BlockSpec((1, d), const),
                pl.BlockSpec((d, n), const)]
    args = [x, g.reshape(1, d), w]
    scratch = []
    if conv is not None:
        width = conv[1] - conv[0]
        in_specs += [pl.BlockSpec((CONV_W, width), const), pl.BlockSpec((1, width), const)]
        args += [conv_w, conv_b.reshape(1, width)]
        scratch = [pltpu.VMEM((8, width), F32)]
    return pl.pallas_call(
        functools.partial(_norm_matmul_body, n_chunk=512, conv=conv,
                          blocks_per_seq=None if conv is None else seq_len // tm),
        grid=(m // tm,),
        in_specs=in_specs,
        out_specs=pl.BlockSpec((tm, n), lambda i: (i, 0)),
        out_shape=jax.ShapeDtypeStruct((m, n), BF16),
        scratch_shapes=scratch,
        compiler_params=_params(1),
        name="norm_matmul",
    )(*args)


def _xattn_block(h, kv_ref, wq_ref, wo_ref, pre_g, post_g):
    hn = _rms(h, pre_g).astype(BF16)
    q = _dot(hn, wq_ref[...])
    scale = XA_DH ** -0.5
    outs = []
    for hd in range(XA_HEADS):
        lo = hd * XA_DH
        qh = q[:, lo:lo + XA_DH].astype(BF16)
        kh = kv_ref[:, lo:lo + XA_DH]
        vh = kv_ref[:, D_MODEL + lo:D_MODEL + lo + XA_DH]
        s = _dot_nt(qh, kh) * scale
        p = jnp.exp(s - jnp.max(s, axis=-1, keepdims=True))
        l = jnp.sum(p, axis=-1, keepdims=True)
        outs.append((_dot(p.astype(BF16), vh) / l).astype(BF16))
    o = jnp.concatenate(outs, axis=-1)
    return h + _rms(_dot(o, wo_ref[...]), post_g)


def _mlp_block(h, w1_ref, w2_ref, pre_g, post_g, ff_chunk):
    hn = _rms(h, pre_g).astype(BF16)
    acc = jnp.zeros(h.shape, F32)
    for c in range(0, w1_ref.shape[1], ff_chunk):
        a = _dot(hn, w1_ref[:, c:c + ff_chunk])
        a = jnp.square(jnp.maximum(a, 0.0)).astype(BF16)
        acc = acc + _dot(a, w2_ref[c:c + ff_chunk, :])
    return h + _rms(acc, post_g)


def _post_body(*refs, n_mix, ff_chunk):
    a_refs = refs[:n_mix]
    wout_refs = refs[n_mix:2 * n_mix]
    (h_ref, kv_ref, wq_ref, wo_ref, w1_ref, w2_ref, gains_ref, o_ref) = refs[2 * n_mix:]
    gain = lambda r: gains_ref[r:r + 1, :]
    acc = _dot(a_refs[0][...], wout_refs[0][...])
    for a_ref, w_ref in zip(a_refs[1:], wout_refs[1:]):
        acc = acc + _dot(a_ref[...], w_ref[...])
    h = h_ref[...] + _rms(acc, gain(0))
    h = _xattn_block(h, kv_ref, wq_ref, wo_ref, gain(1), gain(2))
    o_ref[...] = _mlp_block(h, w1_ref, w2_ref, gain(3), gain(4), ff_chunk)


def post_mixer(mixed_list, wout_list, h, kv, w_q, w_o, w1, w2, gains, seq_len, tm):
    m, d = h.shape
    n_mix = len(mixed_list)
    blocks_per_seq = seq_len // tm
    resident = lambda shape: pl.BlockSpec(shape, lambda i: (0,) * len(shape), pipeline_mode=pl.Buffered(1))
    in_specs = [pl.BlockSpec((tm, a.shape[1]), lambda i: (i, 0)) for a in mixed_list]
    in_specs += [resident(w.shape) for w in wout_list]
    in_specs += [pl.BlockSpec((tm, d), lambda i: (i, 0)),
                 pl.BlockSpec((N_MEM, 2 * d), lambda i: (i // blocks_per_seq, 0)),
                 resident(w_q.shape), resident(w_o.shape), resident(w1.shape), resident(w2.shape),
                 resident(gains.shape)]
    return pl.pallas_call(
        functools.partial(_post_body, n_mix=n_mix, ff_chunk=512),
        grid=(m // tm,),
        in_specs=in_specs,
        out_specs=pl.BlockSpec((tm, d), lambda i: (i, 0)),
        out_shape=jax.ShapeDtypeStruct((m, d), F32),
        compiler_params=_params(1),
        name="post_mixer",
    )(*mixed_list, *wout_list, h, kv, w_q, w_o, w1, w2, gains)


def _gla_constants():
    c = CHUNK
    t = np.arange(c)
    i = t[:, None]
    tt = t[None, :]
    mats = [(tt <= i), (tt > i)]
    masks = []
    for s in GLA_LEVELS:
        sb = i // (2 * s)
        r = sb * 2 * s + s - 1
        right = (i % (2 * s)) >= s
        d = np.where(right, (tt > r) & (tt <= i), (tt > i) & (tt <= r))
        mats.append(d)
        j = tt
        pair = right & ((j // (2 * s)) == sb) & ((j % (2 * s)) < s)
        masks.append(pair)
    masks.append(i == tt)
    dall = np.concatenate(mats, axis=0).astype(np.float32)
    lvl = np.stack([np.tile(m, (1, GLA_HEADS)) for m in masks]).astype(np.float32)
    hk = np.arange(GLA_KEY) // GLA_DK
    hv = np.arange(GLA_VAL) // GLA_DV
    bd_kk = (hk[:, None] == hk[None, :]).astype(np.float32)
    bd_kv = (hk[:, None] == hv[None, :]).astype(np.float32)
    return (jnp.asarray(dall, BF16), jnp.asarray(lvl, F32), jnp.asarray(bd_kk, F32),
            jnp.asarray(bd_kv, BF16), jnp.asarray(bd_kv, F32))


def _gla_body(qk_ref, v_ref, go_ref, lr_ref, w2_ref, b_ref, ng_ref, dall_ref, lvl_ref,
              bdkk_ref, bdkv_ref, bdkvf_ref, o_ref, st_ref, *, n_chunks):
    @pl.when(pl.program_id(1) == 0)
    def _():
        st_ref[...] = jnp.zeros_like(st_ref)

    dall = dall_ref[...]
    bdkk = bdkk_ref[...]
    bdkv = bdkv_ref[...]
    bdkv_f = bdkvf_ref[...]
    n_lvl = len(GLA_LEVELS)

    chunks = range(n_chunks)

    def rows(ci):
        return slice(ci * CHUNK, (ci + 1) * CHUNK)

    z = _dot(lr_ref[...], w2_ref[...]) + b_ref[...]
    gk = -_softplus(-z) * (1.0 / GLA_GATE_NORM)
    g_hi, g_lo = _split2(gk)
    e_all = [jnp.exp(jnp.minimum(_dot(dall, g_hi[rows(ci)]) + _dot(dall, g_lo[rows(ci)]), 0.0))
             for ci in chunks]
    q = [qk_ref[rows(ci), 0:GLA_KEY].astype(F32) * (GLA_DK ** -0.5) for ci in chunks]
    k = [qk_ref[rows(ci), GLA_KEY:2 * GLA_KEY].astype(F32) for ci in chunks]
    v = [v_ref[rows(ci), :] for ci in chunks]

    q_in = [(q[ci] * e_all[ci][0:CHUNK]).astype(BF16) for ci in chunks]
    k_up_t = [(k[ci] * e_all[ci][CHUNK:2 * CHUNK]).T.astype(BF16) for ci in chunks]
    d_col = [e_all[ci][CHUNK - 8:CHUNK].T[:, 7:8] for ci in chunks]
    kv = [bdkv_f * _dot(k_up_t[ci], v[ci]) for ci in chunks]

    a = [jnp.zeros((CHUNK, GLA_KEY), F32) for _ in chunks]
    for li in range(n_lvl + 1):
        for ci in chunks:
            if li < n_lvl:
                e = e_all[ci][(2 + li) * CHUNK:(3 + li) * CHUNK]
                qs, ks = q[ci] * e, k[ci] * e
            else:
                qs, ks = q[ci], k[ci]
            kbd = (jnp.concatenate([ks] * GLA_HEADS, axis=0) * bdkk).astype(BF16)
            a[ci] = a[ci] + _dot_nt(qs.astype(BF16), kbd) * lvl_ref[li]
    o = [_dot(a[ci].astype(BF16), jnp.concatenate([v[ci]] * GLA_HEADS, axis=0) * bdkv) for ci in chunks]

    st = st_ref[...]
    for ci in chunks:
        o[ci] = o[ci] + _dot(q_in[ci], st.astype(BF16))
        st = st * d_col[ci] + kv[ci]
    st_ref[...] = st

    ng = ng_ref[...]
    o = jnp.concatenate(o, axis=0)
    outs = []
    for hd in range(GLA_HEADS):
        sl = slice(hd * GLA_DV, (hd + 1) * GLA_DV)
        outs.append(_rms(o[:, sl], ng) * _silu(go_ref[:, sl].astype(F32)))
    o_ref[...] = jnp.concatenate(outs, axis=-1).astype(o_ref.dtype)


def gla(proj, gk_w2, gk_b, norm_g, batch, seq_len, tb):
    m = proj.shape[0]
    nb = seq_len // tb
    dall, lvl, bdkk, bdkv, bdkv_f = _gla_constants()
    w2 = jnp.zeros((LANE, GLA_KEY), F32).at[:GLA_GATE_RANK].set(gk_w2).astype(BF16)
    row = lambda b, t: (b * nb + t, 0)
    const2 = lambda b, t: (0, 0)
    return pl.pallas_call(
        functools.partial(_gla_body, n_chunks=tb // CHUNK),
        grid=(batch, nb),
        in_specs=[pl.BlockSpec((tb, 2 * GLA_KEY), lambda b, t: (b * nb + t, 0)),
                  pl.BlockSpec((tb, GLA_VAL), lambda b, t: (b * nb + t, 1)),
                  pl.BlockSpec((tb, GLA_VAL), lambda b, t: (b * nb + t, 2)),
                  pl.BlockSpec((tb, LANE), lambda b, t: (b * nb + t, 20)),
                  pl.BlockSpec((LANE, GLA_KEY), const2),
                  pl.BlockSpec((1, GLA_KEY), const2),
                  pl.BlockSpec((1, GLA_DV), const2),
                  pl.BlockSpec(dall.shape, const2),
                  pl.BlockSpec(lvl.shape, lambda b, t: (0, 0, 0)),
                  pl.BlockSpec(bdkk.shape, const2),
                  pl.BlockSpec(bdkv.shape, const2),
                  pl.BlockSpec(bdkv_f.shape, const2)],
        out_specs=pl.BlockSpec((tb, GLA_VAL), row),
        out_shape=jax.ShapeDtypeStruct((m, GLA_VAL), BF16),
        scratch_shapes=[pltpu.VMEM((GLA_KEY, GLA_VAL), F32)],
        compiler_params=_params(2),
        name="gla",
    )(proj, proj, proj, proj, w2, gk_b.reshape(1, GLA_KEY), norm_g.reshape(1, GLA_DV),
      dall, lvl, bdkk, bdkv, bdkv_f)


def _gelu_tanh(x):
    c = np.float32(np.sqrt(2.0 / np.pi))
    return 0.5 * x * (1.0 + jnp.tanh(c * (x + 0.044715 * (x * x * x))))


def _lru_body(x_ref, gate_ref, wa_ref, ba_ref, wx_ref, bx_ref, lam_ref, y_ref, hc_ref, *, tb):
    @pl.when(pl.program_id(1) == 0)
    def _():
        hc_ref[...] = jnp.zeros_like(hc_ref)

    xb = x_ref[...]
    xr = xb.astype(F32)
    ra, ia = [], []
    for n in range(LRU_BLOCKS):
        sl = slice(n * LRU_BLOCK, (n + 1) * LRU_BLOCK)
        ra.append(_dot(xb[:, sl], wa_ref[n]))
        ia.append(_dot(xb[:, sl], wx_ref[n]))
    r = _sigmoid(jnp.concatenate(ra, axis=-1) + ba_ref[...])
    ig = _sigmoid(jnp.concatenate(ia, axis=-1) + bx_ref[...])
    log_a = (-LRU_C) * r * _softplus(-lam_ref[...])
    a = jnp.exp(log_a)
    u = jnp.sqrt(1.0 - jnp.exp(2.0 * log_a)) * (ig * xr)

    rows = lax.broadcasted_iota(jnp.int32, (tb, LRU_WIDTH), 0)
    s = 1
    while s < tb:
        keep = rows >= s
        a_sh = jnp.where(keep, pltpu.roll(a, s, 0), 1.0)
        u_sh = jnp.where(keep, pltpu.roll(u, s, 0), 0.0)
        u = a * u_sh + u
        a = a * a_sh
        s *= 2
    h = a * hc_ref[7:8, :] + u
    hc_ref[...] = h[tb - 8:tb, :]
    y_ref[...] = (h * _gelu_tanh(gate_ref[...].astype(F32))).astype(y_ref.dtype)


def lru(proj, w_a, b_a, w_x, b_x, lam, batch, seq_len, tb):
    m = proj.shape[0]
    nb = seq_len // tb
    w = LRU_WIDTH
    const2 = lambda b, t: (0, 0)
    const3 = lambda b, t: (0, 0, 0)
    return pl.pallas_call(
        functools.partial(_lru_body, tb=tb),
        grid=(batch, nb),
        in_specs=[pl.BlockSpec((tb, w), lambda b, t: (b * nb + t, 3)),
                  pl.BlockSpec((tb, w), lambda b, t: (b * nb + t, 4)),
                  pl.BlockSpec((LRU_BLOCKS, LRU_BLOCK, LRU_BLOCK), const3),
                  pl.BlockSpec((1, w), const2),
                  pl.BlockSpec((LRU_BLOCKS, LRU_BLOCK, LRU_BLOCK), const3),
                  pl.BlockSpec((1, w), const2),
                  pl.BlockSpec((1, w), const2)],
        out_specs=pl.BlockSpec((tb, w), lambda b, t: (b * nb + t, 0)),
        out_shape=jax.ShapeDtypeStruct((m, w), BF16),
        scratch_shapes=[pltpu.VMEM((8, w), F32)],
        compiler_params=_params(2),
        name="lru",
    )(proj, proj, w_a.astype(BF16), b_a.reshape(1, w),
      w_x.astype(BF16), b_x.reshape(1, w), lam.reshape(1, w))


def _gdn_constants():
    c = CHUNK
    t = np.arange(c)
    i = t[:, None]
    j = t[None, :]
    csum = np.concatenate([(j <= i), (j > i)], axis=0).astype(np.float32)
    masks = [(i >= j), (i > j)]
    lv = []
    for s in (1,) + GDN_LEVELS:
        sb = i // (2 * s)
        pair = ((i % (2 * s)) >= s) & ((j // (2 * s)) == sb) & ((j % (2 * s)) < s)
        lv.append(pair)
    masks = np.stack(masks + lv).astype(np.float32)
    eye = np.eye(c, dtype=np.float32)
    return jnp.asarray(csum, BF16), jnp.asarray(masks, F32), jnp.asarray(eye, F32)


def _gdn_body(q_ref, k_ref, v_ref, z_ref, bl_ref, al_ref, alog_ref, dtb_ref, ng_ref,
              csum_ref, masks_ref, eye_ref, o_ref, s_ref, *, tb):
    @pl.when(pl.program_id(1) == 0)
    def _():
        s_ref[...] = jnp.zeros_like(s_ref)

    beta_all = _sigmoid(bl_ref[...].astype(F32))
    g_all = -jnp.exp(alog_ref[...]) * _softplus(al_ref[...].astype(F32) + dtb_ref[...])
    csum = csum_ref[...]
    incl = masks_ref[0]
    strict = masks_ref[1]
    eye = eye_ref[...]
    ng = ng_ref[...]
    scale = GDN_DK ** -0.5

    n_chunks = tb // CHUNK
    heads = range(GDN_HEADS)
    items = [(ci, hd) for ci in range(n_chunks) for hd in heads]

    gcol, grow = [], []
    for ci in range(n_chunks):
        parts = _split3(g_all[ci * CHUNK:(ci + 1) * CHUNK, :])
        gcol.append(sum(_dot(csum, p) for p in parts))
        grow.append(gcol[ci][0:CHUNK, :].T)

    def rows(ci):
        return slice(ci * CHUNK, (ci + 1) * CHUNK)

    def head(ref, ci, hd):
        return ref[rows(ci), hd * GDN_DK:(hd + 1) * GDN_DK].astype(F32)

    qn, kn, vb16, kb, eg, decay, kgt16 = {}, {}, {}, {}, {}, {}, {}
    for it in items:
        ci, hd = it
        qh = head(q_ref, ci, hd)
        kh = head(k_ref, ci, hd)
        qn[it] = qh * (lax.rsqrt(jnp.sum(qh * qh, axis=-1, keepdims=True) + EPS) * scale)
        kn[it] = kh * lax.rsqrt(jnp.sum(kh * kh, axis=-1, keepdims=True) + EPS)
        beta = beta_all[rows(ci), hd:hd + 1]
        gc = gcol[ci][0:CHUNK, hd:hd + 1]
        gs = gcol[ci][CHUNK:2 * CHUNK, hd:hd + 1]
        eg[it] = jnp.exp(gc)
        decay[it] = jnp.exp(jnp.minimum(gc - grow[ci][hd:hd + 1, :], 0.0)) * incl
        kb[it] = kn[it] * beta
        vb16[it] = (head(v_ref, ci, hd) * beta).astype(BF16)
        kgt16[it] = (kn[it] * jnp.exp(gs)).T.astype(BF16)
    k16 = {it: kn[it].astype(BF16) for it in items}
    kk_qk = {it: _dot_nt(jnp.concatenate([kb[it], qn[it]], axis=0).astype(BF16), k16[it]) for it in items}
    m = {it: kk_qk[it][0:CHUNK] * (decay[it] * strict) for it in items}
    aqk16 = {it: (kk_qk[it][CHUNK:2 * CHUNK] * decay[it]).astype(BF16) for it in items}

    tinv = {it: eye - m[it] * masks_ref[2] for it in items}
    for li in range(len(GDN_LEVELS)):
        t16 = {it: tinv[it].astype(BF16) for it in items}
        p16 = {it: _dot(t16[it], (m[it] * masks_ref[3 + li]).astype(BF16)).astype(BF16) for it in items}
        tinv = {it: tinv[it] - _dot(p16[it], t16[it]) for it in items}
    uw = {it: _dot(tinv[it].astype(BF16),
                   jnp.concatenate([vb16[it], (kb[it] * eg[it]).astype(BF16)], axis=1)) for it in items}
    wq16 = {it: jnp.concatenate([uw[it][:, GDN_DV:], qn[it] * eg[it]], axis=0).astype(BF16) for it in items}

    s = {hd: s_ref[hd] for hd in heads}
    for ci in range(n_chunks):
        ws = {hd: _dot(wq16[ci, hd], s[hd].astype(BF16)) for hd in heads}
        vn16 = {hd: (uw[ci, hd][:, :GDN_DV] - ws[hd][0:CHUNK]).astype(BF16) for hd in heads}
        o = {hd: ws[hd][CHUNK:2 * CHUNK] + _dot(aqk16[ci, hd], vn16[hd]) for hd in heads}
        s = {hd: s[hd] * jnp.exp(gcol[ci][CHUNK - 1:CHUNK, hd:hd + 1]) + _dot(kgt16[ci, hd], vn16[hd])
             for hd in heads}
        for hd in heads:
            sl = slice(hd * GDN_DV, (hd + 1) * GDN_DV)
            gate = _silu(z_ref[rows(ci), sl].astype(F32))
            o_ref[rows(ci), sl] = (_rms(o[hd], ng) * gate).astype(o_ref.dtype)
    for hd in heads:
        s_ref[hd] = s[hd]


def gdn(proj, a_log, dt_bias, norm_g, batch, seq_len, tb):
    m = proj.shape[0]
    nb = seq_len // tb
    csum, masks, eye = _gdn_constants()
    pad8 = lambda vec: jnp.zeros((1, LANE), F32).at[0, :GDN_HEADS].set(vec)
    const2 = lambda b, t: (0, 0)
    col = lambda c: (lambda b, t: (b * nb + t, c))
    nk = GDN_KEY // LANE
    return pl.pallas_call(
        functools.partial(_gdn_body, tb=tb),
        grid=(batch, nb),
        in_specs=[pl.BlockSpec((tb, GDN_KEY), col(0)),
                  pl.BlockSpec((tb, GDN_KEY), col(1)),
                  pl.BlockSpec((tb, GDN_VAL), col(2)),
                  pl.BlockSpec((tb, GDN_VAL), col(3)),
                  pl.BlockSpec((tb, LANE), col(4 * nk)),
                  pl.BlockSpec((tb, LANE), col(4 * nk + 1)),
                  pl.BlockSpec((1, LANE), const2),
                  pl.BlockSpec((1, LANE), const2),
                  pl.BlockSpec((1, GDN_DV), const2),
                  pl.BlockSpec(csum.shape, const2),
                  pl.BlockSpec(masks.shape, lambda b, t: (0, 0, 0)),
                  pl.BlockSpec(eye.shape, const2)],
        out_specs=pl.BlockSpec((tb, GDN_VAL), col(0)),
        out_shape=jax.ShapeDtypeStruct((m, GDN_VAL), BF16),
        scratch_shapes=[pltpu.VMEM((GDN_HEADS, GDN_DK, GDN_DV), F32)],
        compiler_params=_params(2),
        name="gdn",
    )(proj, proj, proj, proj, proj, proj, pad8(a_log), pad8(dt_bias),
      norm_g.reshape(1, GDN_DV), csum, masks, eye)


def _pad_cols(w, width):
    return jnp.pad(w, ((0, 0), (0, width - w.shape[1])))


def _even_w_in(w):
    o1 = GLA_KEY
    o2 = o1 + GLA_KEY
    o3 = o2 + GLA_VAL
    o4 = o3 + GLA_VAL
    o5 = o4 + GLA_GATE_RANK
    return jnp.concatenate([w[:, :o4], w[:, o5:], _pad_cols(w[:, o4:o5], LANE)], axis=1).astype(BF16)


def _odd_w_in(w):
    p2 = 3 * GDN_KEY + GDN_VAL
    p3 = p2 + GDN_HEADS
    return jnp.concatenate([w[:, :p2], _pad_cols(w[:, p2:p3], LANE), _pad_cols(w[:, p3:], LANE)],
                           axis=1).astype(BF16)


def _gains(*rows):
    g = jnp.stack(rows)
    return jnp.pad(g, ((0, 8 - g.shape[0]), (0, 0)))


def even_mixer(h, pre_g, w_in, gk_w2, gk_b, gla_g, conv_w, conv_b, w_a, b_a, w_x, b_x, lam,
               batch, seq_len, tm, tb):
    lru_x0 = 2 * GLA_KEY + 2 * GLA_VAL
    proj = norm_matmul(h, pre_g, _even_w_in(w_in), tm, conv=(lru_x0, lru_x0 + LRU_WIDTH, False),
                       conv_w=conv_w, conv_b=conv_b, seq_len=seq_len)
    o = gla(proj, gk_w2, gk_b, gla_g, batch, seq_len, tb)
    y = lru(proj, w_a, b_a, w_x, b_x, lam, batch, seq_len, tb)
    return [o, y]


def odd_mixer(h, pre_g, w_in, conv_w, a_log, dt_bias, norm_g, batch, seq_len, tm, tb):
    n_conv = 2 * GDN_KEY + GDN_VAL
    proj = norm_matmul(h, pre_g, _odd_w_in(w_in), tm, conv=(0, n_conv, True), conv_w=conv_w,
                       conv_b=jnp.zeros((n_conv,), F32), seq_len=seq_len)
    return [gdn(proj, a_log, dt_bias, norm_g, batch, seq_len, tb)]


def kernel(x, mem, mix_pre_g, mix_post_g, ab_w_in, gla_gk_w2, gla_gk_b, gla_norm_g, lru_conv_w, lru_conv_b, lru_w_a, lru_b_a, lru_w_x, lru_b_x, lru_lambda, ab_w_out, gdn_w_in, gdn_conv_w, gdn_a_log, gdn_dt_bias, gdn_norm_g, gdn_w_out, xa_pre_g, xa_mem_g, xa_post_g, xa_w_q, xa_w_kv, xa_w_o, mlp_pre_g, mlp_post_g, mlp_w1, mlp_w2):
    batch, seq_len, d = x.shape
    depth = mix_pre_g.shape[0]
    tm = min(512, seq_len)
    tb_even = min(256, seq_len)
    tb_odd = min(256, seq_len)
    h = x.reshape(batch * seq_len, d)
    mem2 = mem.reshape(batch * mem.shape[1], d)
    for i in range(depth):
        j = i // 2
        if i % 2 == 0:
            mixed = even_mixer(h, mix_pre_g[i], ab_w_in[j], gla_gk_w2[j], gla_gk_b[j], gla_norm_g[j],
                               lru_conv_w[j], lru_conv_b[j], lru_w_a[j], lru_b_a[j], lru_w_x[j], lru_b_x[j],
                               lru_lambda[j], batch, seq_len, tm, tb_even)
            wo = ab_w_out[j].astype(BF16)
            w_out = [wo[:GLA_VAL], wo[GLA_VAL:]]
        else:
            mixed = odd_mixer(h, mix_pre_g[i], gdn_w_in[j], gdn_conv_w[j], gdn_a_log[j], gdn_dt_bias[j],
                              gdn_norm_g[j], batch, seq_len, tm, tb_odd)
            w_out = [gdn_w_out[j].astype(BF16)]
        kv = norm_matmul(mem2, xa_mem_g[i], xa_w_kv[i].astype(BF16), min(512, mem2.shape[0]))
        gains = _gains(mix_post_g[i], xa_pre_g[i], xa_post_g[i], mlp_pre_g[i], mlp_post_g[i])
        h = post_mixer(mixed, w_out, h, kv, xa_w_q[i].astype(BF16), xa_w_o[i].astype(BF16),
                       mlp_w1[i].astype(BF16), mlp_w2[i].astype(BF16), gains, seq_len, tm)
    return h.reshape(batch, seq_len, d)
```

```python
import functools

import numpy as np
import jax
import jax.numpy as jnp
from jax import lax
from jax.experimental import pallas as pl
from jax.experimental.pallas import tpu as pltpu

F32 = jnp.float32
BF16 = jnp.bfloat16

D_MODEL = 1024
N_MEM = 256
EPS = 1e-6
CHUNK = 64
CONV_W = 4
GLA_HEADS = 4
GLA_DK = 64
GLA_DV = 128
GLA_KEY = GLA_HEADS * GLA_DK
GLA_VAL = GLA_HEADS * GLA_DV
GLA_GATE_RANK = 16
GLA_GATE_NORM = 16.0
LRU_WIDTH = D_MODEL // 2
LRU_BLOCKS = 4
LRU_BLOCK = LRU_WIDTH // LRU_BLOCKS
LRU_C = 8.0
GDN_HEADS = 8
GDN_DK = 128
GDN_DV = 128
GDN_KEY = GDN_HEADS * GDN_DK
GDN_VAL = GDN_HEADS * GDN_DV
XA_HEADS = 4
XA_DH = D_MODEL // XA_HEADS
D_FF = 4 * D_MODEL

LANE = 128
VMEM_LIMIT = 56 * 1024 * 1024

GLA_LEVELS = (32, 16, 8, 4, 2, 1)
GDN_LEVELS = (2, 4, 8, 16, 32)

NT_DIMS = (((1,), (1,)), ((), ()))


def _params(n_axes):
    return pltpu.CompilerParams(
        dimension_semantics=("arbitrary",) * n_axes, vmem_limit_bytes=VMEM_LIMIT)


def _rms(x, g):
    ms = jnp.mean(x * x, axis=-1, keepdims=True)
    return x * lax.rsqrt(ms + EPS) * g


def _softplus(x):
    return jnp.maximum(x, 0.0) + jnp.log1p(jnp.exp(-jnp.abs(x)))


def _sigmoid(x):
    return 0.5 + 0.5 * jnp.tanh(0.5 * x)


def _silu(x):
    hx = 0.5 * x
    return hx + hx * jnp.tanh(hx)


def _dot(a, b):
    return jnp.dot(a, b, preferred_element_type=F32)


def _dot_nt(a, b):
    return lax.dot_general(a, b, NT_DIMS, preferred_element_type=F32)


def _split2(x):
    hi = x.astype(BF16)
    lo = (x - hi.astype(F32)).astype(BF16)
    return hi, lo


def _split3(x):
    hi = x.astype(BF16)
    r = x - hi.astype(F32)
    mid = r.astype(BF16)
    lo = (r - mid.astype(F32)).astype(BF16)
    return hi, mid, lo


def _shifted_taps(prev_tail, cur, cw):
    tm, c = cur.shape
    nt = tm // 8
    xt = jnp.concatenate([prev_tail, cur], axis=0).reshape(nt + 1, 8, c)
    sub = lax.broadcasted_iota(jnp.int32, (nt, 8, c), 1)
    y = cw[CONV_W - 1:CONV_W, :] * xt[1:]
    for kk in range(CONV_W - 1):
        s = CONV_W - 1 - kk
        r = pltpu.roll(xt, s, 1)
        y = y + cw[kk:kk + 1, :] * jnp.where(sub < s, r[:-1], r[1:])
    return y.reshape(tm, c)


def _norm_matmul_body(x_ref, g_ref, w_ref, *rest, n_chunk, conv, blocks_per_seq):
    if conv is None:
        (o_ref,) = rest
    else:
        c0, c1, use_silu, has_bias = conv
        if has_bias:
            cw_ref, cb_ref, o_ref, tail_ref = rest
        else:
            cw_ref, o_ref, tail_ref = rest

        @pl.when(pl.program_id(0) % blocks_per_seq == 0)
        def _():
            tail_ref[...] = jnp.zeros_like(tail_ref)

    tm = x_ref.shape[0]
    xn = _rms(x_ref[...], g_ref[...]).astype(BF16)
    n = o_ref.shape[1]
    starts = list(range(0, n, n_chunk))
    width = lambda c: min(n_chunk, n - c)
    res = _dot(xn, w_ref[:, 0:width(0)])
    for idx, c in enumerate(starts):
        cw = width(c)
        nxt = None
        if idx + 1 < len(starts):
            c2 = starts[idx + 1]
            nxt = _dot(xn, w_ref[:, c2:c2 + width(c2)])
        if conv is not None and c0 <= c < c1:
            lc = slice(c - c0, c - c0 + cw)
            y = _shifted_taps(tail_ref[:, lc], res, cw_ref[:, lc])
            if has_bias:
                y = y + cb_ref[:, lc]
            tail_ref[:, lc] = res[tm - 8:tm, :]
            res = _silu(y) if use_silu else y
        o_ref[:, c:c + cw] = res.astype(o_ref.dtype)
        res = nxt


def norm_matmul(x, g, w, tm, conv=None, conv_w=None, conv_b=None, seq_len=None):
    m, d = x.shape
    n = w.shape[1]
    const = lambda i: (0, 0)
    in_specs = [pl.BlockSpec((tm, d), lambda i: (i, 0)),
                pl.BlockSpec((1, d), const),
                pl.BlockSpec((d, n), const)]
    args = [x, g.reshape(1, d), w]
    scratch = []
    if conv is not None:
        width = conv[1] - conv[0]
        conv = (*conv, conv_b is not None)
        in_specs.append(pl.BlockSpec((CONV_W, width), const))
        args.append(conv_w)
        if conv_b is not None:
            in_specs.append(pl.BlockSpec((1, width), const))
            args.append(conv_b.reshape(1, width))
        scratch = [pltpu.VMEM((8, width), F32)]
    return pl.pallas_call(
        functools.partial(_norm_matmul_body, n_chunk=256, conv=conv,
                          blocks_per_seq=None if conv is None else seq_len // tm),
        grid=(m // tm,),
        in_specs=in_specs,
        out_specs=pl.BlockSpec((tm, n), lambda i: (i, 0)),
        out_shape=jax.ShapeDtypeStruct((m, n), BF16),
        scratch_shapes=scratch,
        compiler_params=_params(1),
        name="norm_matmul",
    )(*args)


def _xattn_block(h, kv_ref, wq_ref, wo_ref, pre_g, post_g):
    hn = _rms(h, pre_g).astype(BF16)
    q = _dot(hn, wq_ref[...])
    scale = XA_DH ** -0.5
    outs = []
    for hd in range(XA_HEADS):
        lo = hd * XA_DH
        qh = q[:, lo:lo + XA_DH].astype(BF16)
        kh = kv_ref[:, lo:lo + XA_DH]
        vh = kv_ref[:, D_MODEL + lo:D_MODEL + lo + XA_DH]
        s = _dot_nt(qh, kh) * scale
        p = jnp.exp(s - jnp.max(s, axis=-1, keepdims=True))
        l = jnp.sum(p, axis=-1, keepdims=True)
        outs.append((_dot(p.astype(BF16), vh) / l).astype(BF16))
    o = jnp.concatenate(outs, axis=-1)
    return h + _rms(_dot(o, wo_ref[...]), post_g)


def _mlp_block(h, w1_ref, w2_ref, pre_g, post_g, ff_chunk):
    hn = _rms(h, pre_g).astype(BF16)
    acc = jnp.zeros(h.shape, F32)
    for c in range(0, w1_ref.shape[1], ff_chunk):
        a = _dot(hn, w1_ref[:, c:c + ff_chunk])
        a = jnp.square(jnp.maximum(a, 0.0)).astype(BF16)
        acc = acc + _dot(a, w2_ref[c:c + ff_chunk, :])
    return h + _rms(acc, post_g)


def _post_body(*refs, n_mix, ff_chunk):
    a_refs = refs[:n_mix]
    wout_refs = refs[n_mix:2 * n_mix]
    (h_ref, kv_ref, wq_ref, wo_ref, w1_ref, w2_ref, gains_ref, o_ref) = refs[2 * n_mix:]
    gain = lambda r: gains_ref[r:r + 1, :]
    acc = _dot(a_refs[0][...], wout_refs[0][...])
    for a_ref, w_ref in zip(a_refs[1:], wout_refs[1:]):
        acc = acc + _dot(a_ref[...], w_ref[...])
    h = h_ref[...] + _rms(acc, gain(0))
    h = _xattn_block(h, kv_ref, wq_ref, wo_ref, gain(1), gain(2))
    o_ref[...] = _mlp_block(h, w1_ref, w2_ref, gain(3), gain(4), ff_chunk)


def post_mixer(mixed_list, wout_list, h, kv, w_q, w_o, w1, w2, gains, seq_len, tm):
    m, d = h.shape
    n_mix = len(mixed_list)
    blocks_per_seq = seq_len // tm
    resident = lambda shape: pl.BlockSpec(shape, lambda i: (0,) * len(shape), pipeline_mode=pl.Buffered(1))
    in_specs = [pl.BlockSpec((tm, a.shape[1]), lambda i: (i, 0)) for a in mixed_list]
    in_specs += [resident(w.shape) for w in wout_list]
    in_specs += [pl.BlockSpec((tm, d), lambda i: (i, 0)),
                 pl.BlockSpec((N_MEM, 2 * d), lambda i: (i // blocks_per_seq, 0)),
                 resident(w_q.shape), resident(w_o.shape), resident(w1.shape), resident(w2.shape),
                 resident(gains.shape)]
    return pl.pallas_call(
        functools.partial(_post_body, n_mix=n_mix, ff_chunk=512),
        grid=(m // tm,),
        in_specs=in_specs,
        out_specs=pl.BlockSpec((tm, d), lambda i: (i, 0)),
        out_shape=jax.ShapeDtypeStruct((m, d), F32),
        compiler_params=_params(1),
        name="post_mixer",
    )(*mixed_list, *wout_list, h, kv, w_q, w_o, w1, w2, gains)


def _gla_constants():
    c = CHUNK
    t = np.arange(c)
    i = t[:, None]
    tt = t[None, :]
    mats = [(tt <= i), (tt > i)]
    masks = []
    for s in GLA_LEVELS:
        sb = i // (2 * s)
        r = sb * 2 * s + s - 1
        right = (i % (2 * s)) >= s
        d = np.where(right, (tt > r) & (tt <= i), (tt > i) & (tt <= r))
        mats.append(d)
        j = tt
        pair = right & ((j // (2 * s)) == sb) & ((j % (2 * s)) < s)
        masks.append(pair)
    masks.append(i == tt)
    dall = np.concatenate(mats, axis=0).astype(np.float32)
    lvl = np.stack([np.tile(m, (1, GLA_HEADS)) for m in masks]).astype(np.float32)
    hk = np.arange(GLA_KEY) // GLA_DK
    hv = np.arange(GLA_VAL) // GLA_DV
    bd_kk = (hk[:, None] == hk[None, :]).astype(np.float32)
    bd_kv = (hk[:, None] == hv[None, :]).astype(np.float32)
    return (jnp.asarray(dall, BF16), jnp.asarray(lvl, F32), jnp.asarray(bd_kk, BF16),
            jnp.asarray(bd_kv, BF16), jnp.asarray(bd_kv, F32))


def _gla_body(qk_ref, v_ref, go_ref, lr_ref, w2_ref, b_ref, ng_ref, dall_ref, lvl_ref,
              bdkk_ref, bdkv_ref, bdkvf_ref, o_ref, st_ref, *, n_chunks):
    @pl.when(pl.program_id(1) == 0)
    def _():
        st_ref[...] = jnp.zeros_like(st_ref)

    dall = dall_ref[...]
    bdkk = bdkk_ref[...]
    bdkv = bdkv_ref[...]
    bdkv_f = bdkvf_ref[...]
    n_lvl = len(GLA_LEVELS)

    chunks = range(n_chunks)

    def rows(ci):
        return slice(ci * CHUNK, (ci + 1) * CHUNK)

    z = _dot(lr_ref[...], w2_ref[...]) + b_ref[...]
    gk = -_softplus(-z) * (1.0 / GLA_GATE_NORM)
    g_hi, g_lo = _split2(gk)
    e_all = [jnp.exp(jnp.minimum(_dot(dall, g_hi[rows(ci)]) + _dot(dall, g_lo[rows(ci)]), 0.0))
             for ci in chunks]
    q = [qk_ref[rows(ci), 0:GLA_KEY].astype(F32) * (GLA_DK ** -0.5) for ci in chunks]
    k = [qk_ref[rows(ci), GLA_KEY:2 * GLA_KEY].astype(F32) for ci in chunks]
    v = [v_ref[rows(ci), :] for ci in chunks]

    q_in = [(q[ci] * e_all[ci][0:CHUNK]).astype(BF16) for ci in chunks]
    k_up_t = [(k[ci] * e_all[ci][CHUNK:2 * CHUNK]).T.astype(BF16) for ci in chunks]
    d_col = [e_all[ci][CHUNK - 8:CHUNK].T[:, 7:8] for ci in chunks]
    kv = [bdkv_f * _dot(k_up_t[ci], v[ci]) for ci in chunks]

    a = [jnp.zeros((CHUNK, GLA_KEY), F32) for _ in chunks]
    for li in range(n_lvl + 1):
        for ci in chunks:
            if li < n_lvl:
                e = e_all[ci][(2 + li) * CHUNK:(3 + li) * CHUNK]
                qs, ks = q[ci] * e, k[ci] * e
            else:
                qs, ks = q[ci], k[ci]
            kbd = jnp.concatenate([ks.astype(BF16)] * GLA_HEADS, axis=0) * bdkk
            a[ci] = a[ci] + _dot_nt(qs.astype(BF16), kbd) * lvl_ref[li]
    o = [_dot(a[ci].astype(BF16), jnp.concatenate([v[ci]] * GLA_HEADS, axis=0) * bdkv) for ci in chunks]

    st = st_ref[...]
    for ci in chunks:
        o[ci] = o[ci] + _dot(q_in[ci], st.astype(BF16))
        st = st * d_col[ci] + kv[ci]
    st_ref[...] = st

    ng = ng_ref[...]
    o = jnp.concatenate(o, axis=0)
    outs = []
    for hd in range(GLA_HEADS):
        sl = slice(hd * GLA_DV, (hd + 1) * GLA_DV)
        outs.append(_rms(o[:, sl], ng) * _silu(go_ref[:, sl].astype(F32)))
    o_ref[...] = jnp.concatenate(outs, axis=-1).astype(o_ref.dtype)


def gla(proj, gk_w2, gk_b, norm_g, batch, seq_len, tb):
    m = proj.shape[0]
    nb = seq_len // tb
    dall, lvl, bdkk, bdkv, bdkv_f = _gla_constants()
    w2 = jnp.zeros((LANE, GLA_KEY), F32).at[:GLA_GATE_RANK].set(gk_w2).astype(BF16)
    row = lambda b, t: (b * nb + t, 0)
    const2 = lambda b, t: (0, 0)
    return pl.pallas_call(
        functools.partial(_gla_body, n_chunks=tb // CHUNK),
        grid=(batch, nb),
        in_specs=[pl.BlockSpec((tb, 2 * GLA_KEY), lambda b, t: (b * nb + t, 0)),
                  pl.BlockSpec((tb, GLA_VAL), lambda b, t: (b * nb + t, 1)),
                  pl.BlockSpec((tb, GLA_VAL), lambda b, t: (b * nb + t, 2)),
                  pl.BlockSpec((tb, LANE), lambda b, t: (b * nb + t, 20)),
                  pl.BlockSpec((LANE, GLA_KEY), const2),
                  pl.BlockSpec((1, GLA_KEY), const2),
                  pl.BlockSpec((1, GLA_DV), const2),
                  pl.BlockSpec(dall.shape, const2),
                  pl.BlockSpec(lvl.shape, lambda b, t: (0, 0, 0)),
                  pl.BlockSpec(bdkk.shape, const2),
                  pl.BlockSpec(bdkv.shape, const2),
                  pl.BlockSpec(bdkv_f.shape, const2)],
        out_specs=pl.BlockSpec((tb, GLA_VAL), row),
        out_shape=jax.ShapeDtypeStruct((m, GLA_VAL), BF16),
        scratch_shapes=[pltpu.VMEM((GLA_KEY, GLA_VAL), F32)],
        compiler_params=_params(2),
        name="gla",
    )(proj, proj, proj, proj, w2, gk_b.reshape(1, GLA_KEY), norm_g.reshape(1, GLA_DV),
      dall, lvl, bdkk, bdkv, bdkv_f)


def _gelu_tanh(x):
    c = np.float32(np.sqrt(2.0 / np.pi))
    return 0.5 * x * (1.0 + jnp.tanh(c * (x + 0.044715 * (x * x * x))))


def _lru_body(x_ref, gate_ref, wa_ref, ba_ref, wx_ref, bx_ref, lam_ref, y_ref, hc_ref, *, tb):
    @pl.when(pl.program_id(1) == 0)
    def _():
        hc_ref[...] = jnp.zeros_like(hc_ref)

    xb = x_ref[...]
    xr = xb.astype(F32)
    ra, ia = [], []
    for n in range(LRU_BLOCKS):
        sl = slice(n * LRU_BLOCK, (n + 1) * LRU_BLOCK)
        ra.append(_dot(xb[:, sl], wa_ref[n]))
        ia.append(_dot(xb[:, sl], wx_ref[n]))
    r = _sigmoid(jnp.concatenate(ra, axis=-1) + ba_ref[...])
    ig = _sigmoid(jnp.concatenate(ia, axis=-1) + bx_ref[...])
    log_a = (-LRU_C) * r * _softplus(-lam_ref[...])
    a = jnp.exp(log_a)
    u = jnp.sqrt(1.0 - jnp.exp(2.0 * log_a)) * (ig * xr)

    nt = tb // 8
    a3 = a.reshape(nt, 8, LRU_WIDTH)
    u3 = u.reshape(nt, 8, LRU_WIDTH)
    sub = lax.broadcasted_iota(jnp.int32, (nt, 8, LRU_WIDTH), 1)
    for s in (1, 2, 4):
        keep = sub >= s
        a_sh = jnp.where(keep, pltpu.roll(a3, s, 1), 1.0)
        u_sh = jnp.where(keep, pltpu.roll(u3, s, 1), 0.0)
        u3 = a3 * u_sh + u3
        a3 = a3 * a_sh
    carry = hc_ref[7:8, :]
    tiles = []
    for t in range(nt):
        tiles.append(a3[t] * carry + u3[t])
        carry = tiles[-1][7:8, :]
    h = jnp.concatenate(tiles, axis=0)
    hc_ref[...] = tiles[-1]
    y_ref[...] = (h * _gelu_tanh(gate_ref[...].astype(F32))).astype(y_ref.dtype)


def lru(proj, w_a, b_a, w_x, b_x, lam, batch, seq_len, tb):
    m = proj.shape[0]
    nb = seq_len // tb
    w = LRU_WIDTH
    const2 = lambda b, t: (0, 0)
    const3 = lambda b, t: (0, 0, 0)
    return pl.pallas_call(
        functools.partial(_lru_body, tb=tb),
        grid=(batch, nb),
        in_specs=[pl.BlockSpec((tb, w), lambda b, t: (b * nb + t, 3)),
                  pl.BlockSpec((tb, w), lambda b, t: (b * nb + t, 4)),
                  pl.BlockSpec((LRU_BLOCKS, LRU_BLOCK, LRU_BLOCK), const3),
                  pl.BlockSpec((1, w), const2),
                  pl.BlockSpec((LRU_BLOCKS, LRU_BLOCK, LRU_BLOCK), const3),
                  pl.BlockSpec((1, w), const2),
                  pl.BlockSpec((1, w), const2)],
        out_specs=pl.BlockSpec((tb, w), lambda b, t: (b * nb + t, 0)),
        out_shape=jax.ShapeDtypeStruct((m, w), BF16),
        scratch_shapes=[pltpu.VMEM((8, w), F32)],
        compiler_params=_params(2),
        name="lru",
    )(proj, proj, w_a.astype(BF16), b_a.reshape(1, w),
      w_x.astype(BF16), b_x.reshape(1, w), lam.reshape(1, w))


def _gdn_constants():
    c = CHUNK
    t = np.arange(c)
    i = t[:, None]
    j = t[None, :]
    csum = np.concatenate([(j <= i), (j > i)], axis=0).astype(np.float32)
    masks = [(i >= j), (i > j)]
    lv = []
    for s in (1,) + GDN_LEVELS:
        sb = i // (2 * s)
        pair = ((i % (2 * s)) >= s) & ((j // (2 * s)) == sb) & ((j % (2 * s)) < s)
        lv.append(pair)
    masks = np.stack(masks + lv).astype(np.float32)
    eye = np.eye(c, dtype=np.float32)
    half = np.arange(2 * GDN_DK) // GDN_DK
    ones_bd = (half[:, None] == half[None, :]).astype(np.float32)
    return jnp.asarray(csum, BF16), jnp.asarray(masks, F32), jnp.asarray(eye, F32), jnp.asarray(ones_bd, BF16)


def _gdn_body(q_ref, k_ref, v_ref, z_ref, bl_ref, al_ref, alog_ref, dtb_ref, ng_ref,
              csum_ref, masks_ref, eye_ref, ones_ref, o_ref, s_ref, *, tb):
    @pl.when(pl.program_id(1) == 0)
    def _():
        s_ref[...] = jnp.zeros_like(s_ref)

    beta_all = _sigmoid(bl_ref[...].astype(F32))
    g_all = -jnp.exp(alog_ref[...]) * _softplus(al_ref[...].astype(F32) + dtb_ref[...])
    csum = csum_ref[...]
    incl = masks_ref[0]
    strict = masks_ref[1]
    eye = eye_ref[...]
    ng = ng_ref[...]
    scale = GDN_DK ** -0.5

    n_chunks = tb // CHUNK
    heads = range(GDN_HEADS)
    items = [(ci, hd) for ci in range(n_chunks) for hd in heads]

    gcol, grow = [], []
    for ci in range(n_chunks):
        parts = _split3(g_all[ci * CHUNK:(ci + 1) * CHUNK, :])
        gcol.append(sum(_dot(csum, p) for p in parts))
        grow.append(gcol[ci][0:CHUNK, :].T)

    def rows(ci):
        return slice(ci * CHUNK, (ci + 1) * CHUNK)

    def head(ref, ci, hd):
        return ref[rows(ci), hd * GDN_DK:(hd + 1) * GDN_DK].astype(F32)

    ones_bd = ones_ref[...]
    qraw = {it: head(q_ref, *it) for it in items}
    kraw = {it: head(k_ref, *it) for it in items}
    ssq = {}
    for ci in range(n_chunks):
        sq = jnp.concatenate(
            [jnp.concatenate([qraw[ci, hd] * qraw[ci, hd], kraw[ci, hd] * kraw[ci, hd]], axis=1)
             for hd in heads], axis=0).astype(BF16)
        tot = _dot(sq, ones_bd)
        for hd in heads:
            ssq[ci, hd] = tot[hd * CHUNK:(hd + 1) * CHUNK]

    qn, kn, vb16, kb, eg, decay, kgt16, kk_qk = {}, {}, {}, {}, {}, {}, {}, {}
    for it in items:
        ci, hd = it
        qn[it] = qraw[it] * (lax.rsqrt(ssq[it][:, :GDN_DK] + EPS) * scale)
        kn[it] = kraw[it] * lax.rsqrt(ssq[it][:, GDN_DK:] + EPS)
        beta = beta_all[rows(ci), hd:hd + 1]
        gc = gcol[ci][0:CHUNK, hd:hd + 1]
        gs = gcol[ci][CHUNK:2 * CHUNK, hd:hd + 1]
        eg[it] = jnp.exp(gc)
        decay[it] = jnp.exp(jnp.minimum(gc - grow[ci][hd:hd + 1, :], 0.0)) * incl
        kb[it] = kn[it] * beta
        kk_qk[it] = _dot_nt(jnp.concatenate([kb[it], qn[it]], axis=0).astype(BF16), kn[it].astype(BF16))
        vb16[it] = (head(v_ref, ci, hd) * beta).astype(BF16)
        kgt16[it] = (kn[it] * jnp.exp(gs)).T.astype(BF16)
    m = {it: kk_qk[it][0:CHUNK] * (decay[it] * strict) for it in items}
    aqk16 = {it: (kk_qk[it][CHUNK:2 * CHUNK] * decay[it]).astype(BF16) for it in items}

    tinv = {it: eye - m[it] * masks_ref[2] for it in items}
    for li in range(len(GDN_LEVELS)):
        t16 = {it: tinv[it].astype(BF16) for it in items}
        p16 = {it: _dot(t16[it], (m[it] * masks_ref[3 + li]).astype(BF16)).astype(BF16) for it in items}
        tinv = {it: tinv[it] - _dot(p16[it], t16[it]) for it in items}
    uw = {it: _dot(tinv[it].astype(BF16),
                   jnp.concatenate([vb16[it], (kb[it] * eg[it]).astype(BF16)], axis=1)) for it in items}
    wq16 = {it: jnp.concatenate([uw[it][:, GDN_DV:], qn[it] * eg[it]], axis=0).astype(BF16) for it in items}

    s = {hd: s_ref[hd] for hd in heads}
    for ci in range(n_chunks):
        ws = {hd: _dot(wq16[ci, hd], s[hd].astype(BF16)) for hd in heads}
        vn16 = {hd: (uw[ci, hd][:, :GDN_DV] - ws[hd][0:CHUNK]).astype(BF16) for hd in heads}
        o = {hd: ws[hd][CHUNK:2 * CHUNK] + _dot(aqk16[ci, hd], vn16[hd]) for hd in heads}
        s = {hd: s[hd] * jnp.exp(gcol[ci][CHUNK - 1:CHUNK, hd:hd + 1]) + _dot(kgt16[ci, hd], vn16[hd])
             for hd in heads}
        for hd in heads:
            sl = slice(hd * GDN_DV, (hd + 1) * GDN_DV)
            gate = _silu(z_ref[rows(ci), sl].astype(F32))
            o_ref[rows(ci), sl] = (_rms(o[hd], ng) * gate).astype(o_ref.dtype)
    for hd in heads:
        s_ref[hd] = s[hd]


def gdn(proj, a_log, dt_bias, norm_g, batch, seq_len, tb):
    m = proj.shape[0]
    nb = seq_len // tb
    csum, masks, eye, ones_bd = _gdn_constants()
    pad8 = lambda vec: jnp.zeros((1, LANE), F32).at[0, :GDN_HEADS].set(vec)
    const2 = lambda b, t: (0, 0)
    col = lambda c: (lambda b, t: (b * nb + t, c))
    nk = GDN_KEY // LANE
    return pl.pallas_call(
        functools.partial(_gdn_body, tb=tb),
        grid=(batch, nb),
        in_specs=[pl.BlockSpec((tb, GDN_KEY), col(0)),
                  pl.BlockSpec((tb, GDN_KEY), col(1)),
                  pl.BlockSpec((tb, GDN_VAL), col(2)),
                  pl.BlockSpec((tb, GDN_VAL), col(3)),
                  pl.BlockSpec((tb, LANE), col(4 * nk)),
                  pl.BlockSpec((tb, LANE), col(4 * nk + 1)),
                  pl.BlockSpec((1, LANE), const2),
                  pl.BlockSpec((1, LANE), const2),
                  pl.BlockSpec((1, GDN_DV), const2),
                  pl.BlockSpec(csum.shape, const2),
                  pl.BlockSpec(masks.shape, lambda b, t: (0, 0, 0)),
                  pl.BlockSpec(eye.shape, const2),
                  pl.BlockSpec(ones_bd.shape, const2)],
        out_specs=pl.BlockSpec((tb, GDN_VAL), col(0)),
        out_shape=jax.ShapeDtypeStruct((m, GDN_VAL), BF16),
        scratch_shapes=[pltpu.VMEM((GDN_HEADS, GDN_DK, GDN_DV), F32)],
        compiler_params=_params(2),
        name="gdn",
    )(proj, proj, proj, proj, proj, proj, pad8(a_log), pad8(dt_bias),
      norm_g.reshape(1, GDN_DV), csum, masks, eye, ones_bd)


def _pad_cols(w, width):
    return jnp.pad(w, ((0, 0), (0, width - w.shape[1])))


def _even_w_in(w):
    o1 = GLA_KEY
    o2 = o1 + GLA_KEY
    o3 = o2 + GLA_VAL
    o4 = o3 + GLA_VAL
    o5 = o4 + GLA_GATE_RANK
    return jnp.concatenate([w[:, :o4], w[:, o5:], _pad_cols(w[:, o4:o5], LANE)], axis=1).astype(BF16)


def _odd_w_in(w):
    p2 = 3 * GDN_KEY + GDN_VAL
    p3 = p2 + GDN_HEADS
    return jnp.concatenate([w[:, :p2], _pad_cols(w[:, p2:p3], LANE), _pad_cols(w[:, p3:], LANE)],
                           axis=1).astype(BF16)


def _gains(*rows):
    g = jnp.stack(rows)
    return jnp.pad(g, ((0, 8 - g.shape[0]), (0, 0)))


def even_mixer(h, pre_g, w_in, gk_w2, gk_b, gla_g, conv_w, conv_b, w_a, b_a, w_x, b_x, lam,
               batch, seq_len, tm, tb):
    lru_x0 = 2 * GLA_KEY + 2 * GLA_VAL
    proj = norm_matmul(h, pre_g, _even_w_in(w_in), tm, conv=(lru_x0, lru_x0 + LRU_WIDTH, False),
                       conv_w=conv_w, conv_b=conv_b, seq_len=seq_len)
    o = gla(proj, gk_w2, gk_b, gla_g, batch, seq_len, tb)
    y = lru(proj, w_a, b_a, w_x, b_x, lam, batch, seq_len, tb)
    return [o, y]


def odd_mixer(h, pre_g, w_in, conv_w, a_log, dt_bias, norm_g, batch, seq_len, tm, tb):
    n_conv = 2 * GDN_KEY + GDN_VAL
    proj = norm_matmul(h, pre_g, _odd_w_in(w_in), tm, conv=(0, n_conv, True), conv_w=conv_w,
                       seq_len=seq_len)
    return [gdn(proj, a_log, dt_bias, norm_g, batch, seq_len, tb)]


def kernel(x, mem, mix_pre_g, mix_post_g, ab_w_in, gla_gk_w2, gla_gk_b, gla_norm_g, lru_conv_w, lru_conv_b, lru_w_a, lru_b_a, lru_w_x, lru_b_x, lru_lambda, ab_w_out, gdn_w_in, gdn_conv_w, gdn_a_log, gdn_dt_bias, gdn_norm_g, gdn_w_out, xa_pre_g, xa_mem_g, xa_post_g, xa_w_q, xa_w_kv, xa_w_o, mlp_pre_g, mlp_post_g, mlp_w1, mlp_w2):
    batch, seq_len, d = x.shape
    depth = mix_pre_g.shape[0]
    tm = min(512, seq_len)
    tb_even = min(256, seq_len)
    tb_odd = min(256, seq_len)
    h = x.reshape(batch * seq_len, d)
    mem2 = mem.reshape(batch * mem.shape[1], d)
    for i in range(depth):
        j = i // 2
        if i % 2 == 0:
            mixed = even_mixer(h, mix_pre_g[i], ab_w_in[j], gla_gk_w2[j], gla_gk_b[j], gla_norm_g[j],
                               lru_conv_w[j], lru_conv_b[j], lru_w_a[j], lru_b_a[j], lru_w_x[j], lru_b_x[j],
                               lru_lambda[j], batch, seq_len, tm, tb_even)
            wo = ab_w_out[j].astype(BF16)
            w_out = [wo[:GLA_VAL], wo[GLA_VAL:]]
        else:
            mixed = odd_mixer(h, mix_pre_g[i], gdn_w_in[j], gdn_conv_w[j], gdn_a_log[j], gdn_dt_bias[j],
                              gdn_norm_g[j], batch, seq_len, tm, tb_odd)
            w_out = [gdn_w_out[j].astype(BF16)]
        kv = norm_matmul(mem2, xa_mem_g[i], xa_w_kv[i].astype(BF16), min(512, mem2.shape[0]))
        gains = _gains(mix_post_g[i], xa_pre_g[i], xa_post_g[i], mlp_pre_g[i], mlp_post_g[i])
        h = post_mixer(mixed, w_out, h, kv, xa_w_q[i].astype(BF16), xa_w_o[i].astype(BF16),
                       mlp_w1[i].astype(BF16), mlp_w2[i].astype(BF16), gains, seq_len, tm)
    return h.reshape(batch, seq_len, d)
```

```python
import functools

import numpy as np
import jax
import jax.numpy as jnp
from jax import lax
from jax.experimental import pallas as pl
from jax.experimental.pallas import tpu as pltpu

F32 = jnp.float32
BF16 = jnp.bfloat16

D_MODEL = 1024
N_MEM = 256
EPS = 1e-6
CHUNK = 64
CONV_W = 4
GLA_HEADS = 4
GLA_DK = 64
GLA_DV = 128
GLA_KEY = GLA_HEADS * GLA_DK
GLA_VAL = GLA_HEADS * GLA_DV
GLA_GATE_RANK = 16
GLA_GATE_NORM = 16.0
LRU_WIDTH = D_MODEL // 2
LRU_BLOCKS = 4
LRU_BLOCK = LRU_WIDTH // LRU_BLOCKS
LRU_C = 8.0
GDN_HEADS = 8
GDN_DK = 128
GDN_DV = 128
GDN_KEY = GDN_HEADS * GDN_DK
GDN_VAL = GDN_HEADS * GDN_DV
XA_HEADS = 4
XA_DH = D_MODEL // XA_HEADS
D_FF = 4 * D_MODEL

LANE = 128
VMEM_LIMIT = 56 * 1024 * 1024

GLA_LEVELS = (32, 16, 8, 4, 2, 1)
GDN_LEVELS = (2, 4, 8, 16, 32)

NT_DIMS = (((1,), (1,)), ((), ()))


def _params(n_axes):
    return pltpu.CompilerParams(
        dimension_semantics=("arbitrary",) * n_axes, vmem_limit_bytes=VMEM_LIMIT)


def _rms(x, g):
    ms = jnp.mean(x * x, axis=-1, keepdims=True)
    return x * lax.rsqrt(ms + EPS) * g


def _softplus(x):
    return jnp.maximum(x, 0.0) + jnp.log1p(jnp.exp(-jnp.abs(x)))


def _sigmoid(x):
    return 0.5 + 0.5 * jnp.tanh(0.5 * x)


def _silu(x):
    hx = 0.5 * x
    return hx + hx * jnp.tanh(hx)


def _dot(a, b):
    return jnp.dot(a, b, preferred_element_type=F32)


def _dot_nt(a, b):
    return lax.dot_general(a, b, NT_DIMS, preferred_element_type=F32)


def _split2(x):
    hi = x.astype(BF16)
    lo = (x - hi.astype(F32)).astype(BF16)
    return hi, lo


def _split3(x):
    hi = x.astype(BF16)
    r = x - hi.astype(F32)
    mid = r.astype(BF16)
    lo = (r - mid.astype(F32)).astype(BF16)
    return hi, mid, lo


def _shifted_taps(prev_tail, cur, cw):
    tm, c = cur.shape
    nt = tm // 8
    xt = jnp.concatenate([prev_tail, cur], axis=0).reshape(nt + 1, 8, c)
    sub = lax.broadcasted_iota(jnp.int32, (nt, 8, c), 1)
    y = cw[CONV_W - 1:CONV_W, :] * xt[1:]
    for kk in range(CONV_W - 1):
        s = CONV_W - 1 - kk
        r = pltpu.roll(xt, s, 1)
        y = y + cw[kk:kk + 1, :] * jnp.where(sub < s, r[:-1], r[1:])
    return y.reshape(tm, c)


def _norm_matmul_body(x_ref, g_ref, w_ref, *rest, n_chunk, conv, blocks_per_seq):
    if conv is None:
        (o_ref,) = rest
    else:
        c0, c1, use_silu, has_bias = conv
        if has_bias:
            cw_ref, cb_ref, o_ref, tail_ref = rest
        else:
            cw_ref, o_ref, tail_ref = rest

        @pl.when(pl.program_id(0) % blocks_per_seq == 0)
        def _():
            tail_ref[...] = jnp.zeros_like(tail_ref)

    tm = x_ref.shape[0]
    xn = _rms(x_ref[...], g_ref[...]).astype(BF16)
    n = o_ref.shape[1]
    starts = list(range(0, n, n_chunk))
    width = lambda c: min(n_chunk, n - c)
    res = _dot(xn, w_ref[:, 0:width(0)])
    for idx, c in enumerate(starts):
        cw = width(c)
        nxt = None
        if idx + 1 < len(starts):
            c2 = starts[idx + 1]
            nxt = _dot(xn, w_ref[:, c2:c2 + width(c2)])
        if conv is not None and c0 <= c < c1:
            lc = slice(c - c0, c - c0 + cw)
            y = _shifted_taps(tail_ref[:, lc], res, cw_ref[:, lc])
            if has_bias:
                y = y + cb_ref[:, lc]
            tail_ref[:, lc] = res[tm - 8:tm, :]
            res = _silu(y) if use_silu else y
        o_ref[:, c:c + cw] = res.astype(o_ref.dtype)
        res = nxt


def norm_matmul(x, g, w, layer, tm, conv=None, conv_w=None, conv_b=None, seq_len=None):
    m, d = x.shape
    n = w.shape[2]
    const = lambda i: (0, 0)
    in_specs = [pl.BlockSpec((tm, d), lambda i: (i, 0)),
                pl.BlockSpec((1, d), const),
                pl.BlockSpec((None, d, n), lambda i: (layer, 0, 0))]
    args = [x, g.reshape(1, d), w]
    scratch = []
    if conv is not None:
        width = conv[1] - conv[0]
        conv = (*conv, conv_b is not None)
        in_specs.append(pl.BlockSpec((CONV_W, width), const))
        args.append(conv_w)
        if conv_b is not None:
            in_specs.append(pl.BlockSpec((1, width), const))
            args.append(conv_b.reshape(1, width))
        scratch = [pltpu.VMEM((8, width), F32)]
    return pl.pallas_call(
        functools.partial(_norm_matmul_body, n_chunk=256, conv=conv,
                          blocks_per_seq=None if conv is None else seq_len // tm),
        grid=(m // tm,),
        in_specs=in_specs,
        out_specs=pl.BlockSpec((tm, n), lambda i: (i, 0)),
        out_shape=jax.ShapeDtypeStruct((m, n), BF16),
        scratch_shapes=scratch,
        compiler_params=_params(1),
        name="norm_matmul",
    )(*args)


def _xattn_block(h, kv_ref, wq_ref, wo_ref, pre_g, post_g):
    hn = _rms(h, pre_g).astype(BF16)
    q = _dot(hn, wq_ref[...])
    scale = XA_DH ** -0.5
    outs = []
    for hd in range(XA_HEADS):
        lo = hd * XA_DH
        qh = q[:, lo:lo + XA_DH].astype(BF16)
        kh = kv_ref[:, lo:lo + XA_DH]
        vh = kv_ref[:, D_MODEL + lo:D_MODEL + lo + XA_DH]
        s = _dot_nt(qh, kh) * scale
        p = jnp.exp(s - jnp.max(s, axis=-1, keepdims=True))
        l = jnp.sum(p, axis=-1, keepdims=True)
        outs.append((_dot(p.astype(BF16), vh) / l).astype(BF16))
    o = jnp.concatenate(outs, axis=-1)
    return h + _rms(_dot(o, wo_ref[...]), post_g)


def _mlp_block(h, w1_ref, w2_ref, pre_g, post_g, ff_chunk):
    hn = _rms(h, pre_g).astype(BF16)
    acc = jnp.zeros(h.shape, F32)
    for c in range(0, w1_ref.shape[1], ff_chunk):
        a = _dot(hn, w1_ref[:, c:c + ff_chunk])
        a = jnp.square(jnp.maximum(a, 0.0)).astype(BF16)
        acc = acc + _dot(a, w2_ref[c:c + ff_chunk, :])
    return h + _rms(acc, post_g)


def _post_body(*refs, n_mix, ff_chunk):
    a_refs = refs[:n_mix]
    wout_refs = refs[n_mix:2 * n_mix]
    (h_ref, kv_ref, wq_ref, wo_ref, w1_ref, w2_ref, gains_ref, o_ref) = refs[2 * n_mix:]
    gain = lambda r: gains_ref[r:r + 1, :]
    acc = _dot(a_refs[0][...], wout_refs[0][...])
    for a_ref, w_ref in zip(a_refs[1:], wout_refs[1:]):
        acc = acc + _dot(a_ref[...], w_ref[...])
    h = h_ref[...] + _rms(acc, gain(0))
    h = _xattn_block(h, kv_ref, wq_ref, wo_ref, gain(1), gain(2))
    o_ref[...] = _mlp_block(h, w1_ref, w2_ref, gain(3), gain(4), ff_chunk)


def post_mixer(mixed_list, w_out, out_layer, h, kv, w_q, w_o, w1, w2, layer, gains, seq_len, tm):
    m, d = h.shape
    n_mix = len(mixed_list)
    blocks_per_seq = seq_len // tm

    def resident(w, lyr, row_block=0, rows=None):
        rows = w.shape[1] if rows is None else rows
        return pl.BlockSpec((None, rows, w.shape[2]), lambda i: (lyr, row_block, 0),
                            pipeline_mode=pl.Buffered(1))

    in_specs = [pl.BlockSpec((tm, a.shape[1]), lambda i: (i, 0)) for a in mixed_list]
    in_specs += [resident(w_out, out_layer, k, a.shape[1]) for k, a in enumerate(mixed_list)]
    in_specs += [pl.BlockSpec((tm, d), lambda i: (i, 0)),
                 pl.BlockSpec((N_MEM, 2 * d), lambda i: (i // blocks_per_seq, 0)),
                 resident(w_q, layer), resident(w_o, layer), resident(w1, layer), resident(w2, layer),
                 pl.BlockSpec(gains.shape, lambda i: (0, 0), pipeline_mode=pl.Buffered(1))]
    wout_list = [w_out] * n_mix
    return pl.pallas_call(
        functools.partial(_post_body, n_mix=n_mix, ff_chunk=512),
        grid=(m // tm,),
        in_specs=in_specs,
        out_specs=pl.BlockSpec((tm, d), lambda i: (i, 0)),
        out_shape=jax.ShapeDtypeStruct((m, d), F32),
        compiler_params=_params(1),
        name="post_mixer",
    )(*mixed_list, *wout_list, h, kv, w_q, w_o, w1, w2, gains)


def _gla_constants():
    c = CHUNK
    t = np.arange(c)
    i = t[:, None]
    tt = t[None, :]
    mats = [(tt <= i), (tt > i)]
    masks = []
    for s in GLA_LEVELS:
        sb = i // (2 * s)
        r = sb * 2 * s + s - 1
        right = (i % (2 * s)) >= s
        d = np.where(right, (tt > r) & (tt <= i), (tt > i) & (tt <= r))
        mats.append(d)
        j = tt
        pair = right & ((j // (2 * s)) == sb) & ((j % (2 * s)) < s)
        masks.append(pair)
    masks.append(i == tt)
    dall = np.concatenate(mats, axis=0).astype(np.float32)
    lvl = np.stack([np.tile(m, (1, GLA_HEADS)) for m in masks]).astype(np.float32)
    hk = np.arange(GLA_KEY) // GLA_DK
    hv = np.arange(GLA_VAL) // GLA_DV
    bd_kk = (hk[:, None] == hk[None, :]).astype(np.float32)
    bd_kv = (hk[:, None] == hv[None, :]).astype(np.float32)
    return (jnp.asarray(dall, BF16), jnp.asarray(lvl, F32), jnp.asarray(bd_kk, BF16),
            jnp.asarray(bd_kv, BF16), jnp.asarray(bd_kv, F32))


def _gla_body(qk_ref, v_ref, go_ref, lr_ref, w2_ref, b_ref, ng_ref, dall_ref, lvl_ref,
              bdkk_ref, bdkv_ref, bdkvf_ref, o_ref, st_ref, *, n_chunks):
    @pl.when(pl.program_id(1) == 0)
    def _():
        st_ref[...] = jnp.zeros_like(st_ref)

    dall = dall_ref[...]
    bdkk = bdkk_ref[...]
    bdkv = bdkv_ref[...]
    bdkv_f = bdkvf_ref[...]
    n_lvl = len(GLA_LEVELS)

    chunks = range(n_chunks)

    def rows(ci):
        return slice(ci * CHUNK, (ci + 1) * CHUNK)

    z = _dot(lr_ref[...], w2_ref[...]) + b_ref[...]
    gk = -_softplus(-z) * (1.0 / GLA_GATE_NORM)
    g_hi, g_lo = _split2(gk)
    e_all = [jnp.exp(jnp.minimum(_dot(dall, g_hi[rows(ci)]) + _dot(dall, g_lo[rows(ci)]), 0.0))
             for ci in chunks]
    q = [qk_ref[rows(ci), 0:GLA_KEY].astype(F32) * (GLA_DK ** -0.5) for ci in chunks]
    k = [qk_ref[rows(ci), GLA_KEY:2 * GLA_KEY].astype(F32) for ci in chunks]
    v = [v_ref[rows(ci), :] for ci in chunks]

    q_in = [(q[ci] * e_all[ci][0:CHUNK]).astype(BF16) for ci in chunks]
    k_up_t = [(k[ci] * e_all[ci][CHUNK:2 * CHUNK]).T.astype(BF16) for ci in chunks]
    d_col = [e_all[ci][CHUNK - 8:CHUNK].T[:, 7:8] for ci in chunks]
    kv = [bdkv_f * _dot(k_up_t[ci], v[ci]) for ci in chunks]

    a = [jnp.zeros((CHUNK, GLA_KEY), F32) for _ in chunks]
    for li in range(n_lvl + 1):
        for ci in chunks:
            if li < n_lvl:
                e = e_all[ci][(2 + li) * CHUNK:(3 + li) * CHUNK]
                qs, ks = q[ci] * e, k[ci] * e
            else:
                qs, ks = q[ci], k[ci]
            kbd = jnp.concatenate([ks.astype(BF16)] * GLA_HEADS, axis=0) * bdkk
            a[ci] = a[ci] + _dot_nt(qs.astype(BF16), kbd) * lvl_ref[li]
    o = [_dot(a[ci].astype(BF16), jnp.concatenate([v[ci]] * GLA_HEADS, axis=0) * bdkv) for ci in chunks]

    st = st_ref[...]
    for ci in chunks:
        o[ci] = o[ci] + _dot(q_in[ci], st.astype(BF16))
        st = st * d_col[ci] + kv[ci]
    st_ref[...] = st

    ng = ng_ref[...]
    o = jnp.concatenate(o, axis=0)
    outs = []
    for hd in range(GLA_HEADS):
        sl = slice(hd * GLA_DV, (hd + 1) * GLA_DV)
        outs.append(_rms(o[:, sl], ng) * _silu(go_ref[:, sl].astype(F32)))
    o_ref[...] = jnp.concatenate(outs, axis=-1).astype(o_ref.dtype)


def gla(proj, gk_w2, gk_b, norm_g, batch, seq_len, tb):
    m = proj.shape[0]
    nb = seq_len // tb
    dall, lvl, bdkk, bdkv, bdkv_f = _gla_constants()
    w2 = jnp.zeros((LANE, GLA_KEY), F32).at[:GLA_GATE_RANK].set(gk_w2).astype(BF16)
    row = lambda b, t: (b * nb + t, 0)
    const2 = lambda b, t: (0, 0)
    return pl.pallas_call(
        functools.partial(_gla_body, n_chunks=tb // CHUNK),
        grid=(batch, nb),
        in_specs=[pl.BlockSpec((tb, 2 * GLA_KEY), lambda b, t: (b * nb + t, 0)),
                  pl.BlockSpec((tb, GLA_VAL), lambda b, t: (b * nb + t, 1)),
                  pl.BlockSpec((tb, GLA_VAL), lambda b, t: (b * nb + t, 2)),
                  pl.BlockSpec((tb, LANE), lambda b, t: (b * nb + t, 20)),
                  pl.BlockSpec((LANE, GLA_KEY), const2),
                  pl.BlockSpec((1, GLA_KEY), const2),
                  pl.BlockSpec((1, GLA_DV), const2),
                  pl.BlockSpec(dall.shape, const2),
                  pl.BlockSpec(lvl.shape, lambda b, t: (0, 0, 0)),
                  pl.BlockSpec(bdkk.shape, const2),
                  pl.BlockSpec(bdkv.shape, const2),
                  pl.BlockSpec(bdkv_f.shape, const2)],
        out_specs=pl.BlockSpec((tb, GLA_VAL), row),
        out_shape=jax.ShapeDtypeStruct((m, GLA_VAL), BF16),
        scratch_shapes=[pltpu.VMEM((GLA_KEY, GLA_VAL), F32)],
        compiler_params=_params(2),
        name="gla",
    )(proj, proj, proj, proj, w2, gk_b.reshape(1, GLA_KEY), norm_g.reshape(1, GLA_DV),
      dall, lvl, bdkk, bdkv, bdkv_f)


def _gelu_tanh(x):
    c = np.float32(np.sqrt(2.0 / np.pi))
    return 0.5 * x * (1.0 + jnp.tanh(c * (x + 0.044715 * (x * x * x))))


def _lru_body(x_ref, gate_ref, wa_ref, ba_ref, wx_ref, bx_ref, lam_ref, y_ref, hc_ref, *, tb):
    @pl.when(pl.program_id(1) == 0)
    def _():
        hc_ref[...] = jnp.zeros_like(hc_ref)

    xb = x_ref[...]
    xr = xb.astype(F32)
    ra, ia = [], []
    for n in range(LRU_BLOCKS):
        sl = slice(n * LRU_BLOCK, (n + 1) * LRU_BLOCK)
        ra.append(_dot(xb[:, sl], wa_ref[n]))
        ia.append(_dot(xb[:, sl], wx_ref[n]))
    r = _sigmoid(jnp.concatenate(ra, axis=-1) + ba_ref[...])
    ig = _sigmoid(jnp.concatenate(ia, axis=-1) + bx_ref[...])
    log_a = (-LRU_C) * r * _softplus(-lam_ref[...])
    a = jnp.exp(log_a)
    u = jnp.sqrt(1.0 - jnp.exp(2.0 * log_a)) * (ig * xr)

    nt = tb // 8
    a3 = a.reshape(nt, 8, LRU_WIDTH)
    u3 = u.reshape(nt, 8, LRU_WIDTH)
    sub = lax.broadcasted_iota(jnp.int32, (nt, 8, LRU_WIDTH), 1)
    for s in (1, 2, 4):
        keep = sub >= s
        a_sh = jnp.where(keep, pltpu.roll(a3, s, 1), 1.0)
        u_sh = jnp.where(keep, pltpu.roll(u3, s, 1), 0.0)
        u3 = a3 * u_sh + u3
        a3 = a3 * a_sh
    carry = hc_ref[7:8, :]
    tiles = []
    for t in range(nt):
        tiles.append(a3[t] * carry + u3[t])
        carry = tiles[-1][7:8, :]
    h = jnp.concatenate(tiles, axis=0)
    hc_ref[...] = tiles[-1]
    y_ref[...] = (h * _gelu_tanh(gate_ref[...].astype(F32))).astype(y_ref.dtype)


def lru(proj, w_a, b_a, w_x, b_x, lam, batch, seq_len, tb):
    m = proj.shape[0]
    nb = seq_len // tb
    w = LRU_WIDTH
    const2 = lambda b, t: (0, 0)
    const3 = lambda b, t: (0, 0, 0)
    return pl.pallas_call(
        functools.partial(_lru_body, tb=tb),
        grid=(batch, nb),
        in_specs=[pl.BlockSpec((tb, w), lambda b, t: (b * nb + t, 3)),
                  pl.BlockSpec((tb, w), lambda b, t: (b * nb + t, 4)),
                  pl.BlockSpec((LRU_BLOCKS, LRU_BLOCK, LRU_BLOCK), const3),
                  pl.BlockSpec((1, w), const2),
                  pl.BlockSpec((LRU_BLOCKS, LRU_BLOCK, LRU_BLOCK), const3),
                  pl.BlockSpec((1, w), const2),
                  pl.BlockSpec((1, w), const2)],
        out_specs=pl.BlockSpec((tb, w), lambda b, t: (b * nb + t, 0)),
        out_shape=jax.ShapeDtypeStruct((m, w), BF16),
        scratch_shapes=[pltpu.VMEM((8, w), F32)],
        compiler_params=_params(2),
        name="lru",
    )(proj, proj, w_a.astype(BF16), b_a.reshape(1, w),
      w_x.astype(BF16), b_x.reshape(1, w), lam.reshape(1, w))


def _gdn_constants():
    c = CHUNK
    t = np.arange(c)
    i = t[:, None]
    j = t[None, :]
    csum = np.concatenate([(j <= i), (j > i)], axis=0).astype(np.float32)
    masks = [(i >= j), (i > j)]
    lv = []
    for s in (1,) + GDN_LEVELS:
        sb = i // (2 * s)
        pair = ((i % (2 * s)) >= s) & ((j // (2 * s)) == sb) & ((j % (2 * s)) < s)
        lv.append(pair)
    masks = np.stack(masks + lv).astype(np.float32)
    eye = np.eye(c, dtype=np.float32)
    half = np.arange(2 * GDN_DK) // GDN_DK
    ones_bd = (half[:, None] == half[None, :]).astype(np.float32)
    return jnp.asarray(csum, BF16), jnp.asarray(masks, F32), jnp.asarray(eye, F32), jnp.asarray(ones_bd, BF16)


def _gdn_body(q_ref, k_ref, v_ref, z_ref, bl_ref, al_ref, alog_ref, dtb_ref, ng_ref,
              csum_ref, masks_ref, eye_ref, ones_ref, o_ref, s_ref, *, tb):
    @pl.when(pl.program_id(1) == 0)
    def _():
        s_ref[...] = jnp.zeros_like(s_ref)

    beta_all = _sigmoid(bl_ref[...].astype(F32))
    g_all = -jnp.exp(alog_ref[...]) * _softplus(al_ref[...].astype(F32) + dtb_ref[...])
    csum = csum_ref[...]
    incl = masks_ref[0]
    strict = masks_ref[1]
    eye = eye_ref[...]
    ng = ng_ref[...]
    scale = GDN_DK ** -0.5

    n_chunks = tb // CHUNK
    heads = range(GDN_HEADS)
    items = [(ci, hd) for ci in range(n_chunks) for hd in heads]

    gcol, grow = [], []
    for ci in range(n_chunks):
        parts = _split3(g_all[ci * CHUNK:(ci + 1) * CHUNK, :])
        gcol.append(sum(_dot(csum, p) for p in parts))
        grow.append(gcol[ci][0:CHUNK, :].T)

    def rows(ci):
        return slice(ci * CHUNK, (ci + 1) * CHUNK)

    def head(ref, ci, hd):
        return ref[rows(ci), hd * GDN_DK:(hd + 1) * GDN_DK].astype(F32)

    ones_bd = ones_ref[...]
    qraw = {it: head(q_ref, *it) for it in items}
    kraw = {it: head(k_ref, *it) for it in items}
    ssq = {}
    for ci in range(n_chunks):
        sq = jnp.concatenate(
            [jnp.concatenate([qraw[ci, hd] * qraw[ci, hd], kraw[ci, hd] * kraw[ci, hd]], axis=1)
             for hd in heads], axis=0).astype(BF16)
        tot = _dot(sq, ones_bd)
        for hd in heads:
            ssq[ci, hd] = tot[hd * CHUNK:(hd + 1) * CHUNK]

    qn, kn, vb16, kb, eg, decay, kgt16, kk_qk = {}, {}, {}, {}, {}, {}, {}, {}
    for it in items:
        ci, hd = it
        qn[it] = qraw[it] * (lax.rsqrt(ssq[it][:, :GDN_DK] + EPS) * scale)
        kn[it] = kraw[it] * lax.rsqrt(ssq[it][:, GDN_DK:] + EPS)
        beta = beta_all[rows(ci), hd:hd + 1]
        gc = gcol[ci][0:CHUNK, hd:hd + 1]
        gs = gcol[ci][CHUNK:2 * CHUNK, hd:hd + 1]
        eg[it] = jnp.exp(gc)
        decay[it] = jnp.exp(jnp.minimum(gc - grow[ci][hd:hd + 1, :], 0.0)) * incl
        kb[it] = kn[it] * beta
        kk_qk[it] = _dot_nt(jnp.concatenate([kb[it], qn[it]], axis=0).astype(BF16), kn[it].astype(BF16))
        vb16[it] = (head(v_ref, ci, hd) * beta).astype(BF16)
        kgt16[it] = (kn[it] * jnp.exp(gs)).T.astype(BF16)
    m = {it: kk_qk[it][0:CHUNK] * (decay[it] * strict) for it in items}
    aqk16 = {it: (kk_qk[it][CHUNK:2 * CHUNK] * decay[it]).astype(BF16) for it in items}

    tinv = {it: eye - m[it] * masks_ref[2] for it in items}
    for li in range(len(GDN_LEVELS)):
        t16 = {it: tinv[it].astype(BF16) for it in items}
        p16 = {it: _dot(t16[it], (m[it] * masks_ref[3 + li]).astype(BF16)).astype(BF16) for it in items}
        tinv = {it: tinv[it] - _dot(p16[it], t16[it]) for it in items}
    uw = {it: _dot(tinv[it].astype(BF16),
                   jnp.concatenate([vb16[it], (kb[it] * eg[it]).astype(BF16)], axis=1)) for it in items}
    wq16 = {it: jnp.concatenate([uw[it][:, GDN_DV:], qn[it] * eg[it]], axis=0).astype(BF16) for it in items}

    s = {hd: s_ref[hd] for hd in heads}
    for ci in range(n_chunks):
        ws = {hd: _dot(wq16[ci, hd], s[hd].astype(BF16)) for hd in heads}
        vn16 = {hd: (uw[ci, hd][:, :GDN_DV] - ws[hd][0:CHUNK]).astype(BF16) for hd in heads}
        o = {hd: ws[hd][CHUNK:2 * CHUNK] + _dot(aqk16[ci, hd], vn16[hd]) for hd in heads}
        s = {hd: s[hd] * jnp.exp(gcol[ci][CHUNK - 1:CHUNK, hd:hd + 1]) + _dot(kgt16[ci, hd], vn16[hd])
             for hd in heads}
        for hd in heads:
            sl = slice(hd * GDN_DV, (hd + 1) * GDN_DV)
            gate = _silu(z_ref[rows(ci), sl].astype(F32))
            o_ref[rows(ci), sl] = (_rms(o[hd], ng) * gate).astype(o_ref.dtype)
    for hd in heads:
        s_ref[hd] = s[hd]


def gdn(proj, a_log, dt_bias, norm_g, batch, seq_len, tb):
    m = proj.shape[0]
    nb = seq_len // tb
    csum, masks, eye, ones_bd = _gdn_constants()
    pad8 = lambda vec: jnp.zeros((1, LANE), F32).at[0, :GDN_HEADS].set(vec)
    const2 = lambda b, t: (0, 0)
    col = lambda c: (lambda b, t: (b * nb + t, c))
    nk = GDN_KEY // LANE
    return pl.pallas_call(
        functools.partial(_gdn_body, tb=tb),
        grid=(batch, nb),
        in_specs=[pl.BlockSpec((tb, GDN_KEY), col(0)),
                  pl.BlockSpec((tb, GDN_KEY), col(1)),
                  pl.BlockSpec((tb, GDN_VAL), col(2)),
                  pl.BlockSpec((tb, GDN_VAL), col(3)),
                  pl.BlockSpec((tb, LANE), col(4 * nk)),
                  pl.BlockSpec((tb, LANE), col(4 * nk + 1)),
                  pl.BlockSpec((1, LANE), const2),
                  pl.BlockSpec((1, LANE), const2),
                  pl.BlockSpec((1, GDN_DV), const2),
                  pl.BlockSpec(csum.shape, const2),
                  pl.BlockSpec(masks.shape, lambda b, t: (0, 0, 0)),
                  pl.BlockSpec(eye.shape, const2),
                  pl.BlockSpec(ones_bd.shape, const2)],
        out_specs=pl.BlockSpec((tb, GDN_VAL), col(0)),
        out_shape=jax.ShapeDtypeStruct((m, GDN_VAL), BF16),
        scratch_shapes=[pltpu.VMEM((GDN_HEADS, GDN_DK, GDN_DV), F32)],
        compiler_params=_params(2),
        name="gdn",
    )(proj, proj, proj, proj, proj, proj, pad8(a_log), pad8(dt_bias),
      norm_g.reshape(1, GDN_DV), csum, masks, eye, ones_bd)


def _pad_cols(w, width):
    return jnp.pad(w, ((0, 0),) * (w.ndim - 1) + ((0, width - w.shape[-1]),))


def _even_w_in(w):
    o1 = GLA_KEY
    o2 = o1 + GLA_KEY
    o3 = o2 + GLA_VAL
    o4 = o3 + GLA_VAL
    o5 = o4 + GLA_GATE_RANK
    return jnp.concatenate([w[..., :o4], w[..., o5:], _pad_cols(w[..., o4:o5], LANE)], axis=-1).astype(BF16)


def _odd_w_in(w):
    p2 = 3 * GDN_KEY + GDN_VAL
    p3 = p2 + GDN_HEADS
    return jnp.concatenate([w[..., :p2], _pad_cols(w[..., p2:p3], LANE), _pad_cols(w[..., p3:], LANE)],
                           axis=-1).astype(BF16)


def _gains(*rows):
    g = jnp.stack(rows)
    return jnp.pad(g, ((0, 8 - g.shape[0]), (0, 0)))


def even_mixer(h, pre_g, w_in, layer, gk_w2, gk_b, gla_g, conv_w, conv_b, w_a, b_a, w_x, b_x, lam,
               batch, seq_len, tm, tb):
    lru_x0 = 2 * GLA_KEY + 2 * GLA_VAL
    proj = norm_matmul(h, pre_g, w_in, layer, tm, conv=(lru_x0, lru_x0 + LRU_WIDTH, False),
                       conv_w=conv_w, conv_b=conv_b, seq_len=seq_len)
    o = gla(proj, gk_w2, gk_b, gla_g, batch, seq_len, tb)
    y = lru(proj, w_a, b_a, w_x, b_x, lam, batch, seq_len, tb)
    return [o, y]


def odd_mixer(h, pre_g, w_in, layer, conv_w, a_log, dt_bias, norm_g, batch, seq_len, tm, tb):
    n_conv = 2 * GDN_KEY + GDN_VAL
    proj = norm_matmul(h, pre_g, w_in, layer, tm, conv=(0, n_conv, True), conv_w=conv_w, seq_len=seq_len)
    return [gdn(proj, a_log, dt_bias, norm_g, batch, seq_len, tb)]


def kernel(x, mem, mix_pre_g, mix_post_g, ab_w_in, gla_gk_w2, gla_gk_b, gla_norm_g, lru_conv_w, lru_conv_b, lru_w_a, lru_b_a, lru_w_x, lru_b_x, lru_lambda, ab_w_out, gdn_w_in, gdn_conv_w, gdn_a_log, gdn_dt_bias, gdn_norm_g, gdn_w_out, xa_pre_g, xa_mem_g, xa_post_g, xa_w_q, xa_w_kv, xa_w_o, mlp_pre_g, mlp_post_g, mlp_w1, mlp_w2):
    batch, seq_len, d = x.shape
    depth = mix_pre_g.shape[0]
    tm = min(512, seq_len)
    tb_even = min(256, seq_len)
    tb_odd = min(256, seq_len)
    h = x.reshape(batch * seq_len, d)
    mem2 = mem.reshape(batch * mem.shape[1], d)
    even_w_in = _even_w_in(ab_w_in)
    odd_w_in = _odd_w_in(gdn_w_in)
    even_w_out = ab_w_out.astype(BF16)
    odd_w_out = gdn_w_out.astype(BF16)
    w_q, w_kv, w_o = xa_w_q.astype(BF16), xa_w_kv.astype(BF16), xa_w_o.astype(BF16)
    w1, w2 = mlp_w1.astype(BF16), mlp_w2.astype(BF16)
    for i in range(depth):
        j = i // 2
        if i % 2 == 0:
            mixed = even_mixer(h, mix_pre_g[i], even_w_in, j, gla_gk_w2[j], gla_gk_b[j], gla_norm_g[j],
                               lru_conv_w[j], lru_conv_b[j], lru_w_a[j], lru_b_a[j], lru_w_x[j], lru_b_x[j],
                               lru_lambda[j], batch, seq_len, tm, tb_even)
            w_out = even_w_out
        else:
            mixed = odd_mixer(h, mix_pre_g[i], odd_w_in, j, gdn_conv_w[j], gdn_a_log[j], gdn_dt_bias[j],
                              gdn_norm_g[j], batch, seq_len, tm, tb_odd)
            w_out = odd_w_out
        kv = norm_matmul(mem2, xa_mem_g[i], w_kv, i, min(512, mem2.shape[0]))
        gains = _gains(mix_post_g[i], xa_pre_g[i], xa_post_g[i], mlp_pre_g[i], mlp_post_g[i])
        h = post_mixer(mixed, w_out, j, h, kv, w_q, w_o, w1, w2, i, gains, seq_len, tm)
    return h.reshape(batch, seq_len, d)
```

```python
import functools

import numpy as np
import jax
import jax.numpy as jnp
from jax import lax
from jax.experimental import pallas as pl
from jax.experimental.pallas import tpu as pltpu

F32 = jnp.float32
BF16 = jnp.bfloat16

D_MODEL = 1024
N_MEM = 256
EPS = 1e-6
CHUNK = 64
CONV_W = 4
GLA_HEADS = 4
GLA_DK = 64
GLA_DV = 128
GLA_KEY = GLA_HEADS * GLA_DK
GLA_VAL = GLA_HEADS * GLA_DV
GLA_GATE_RANK = 16
GLA_GATE_NORM = 16.0
LRU_WIDTH = D_MODEL // 2
LRU_BLOCKS = 4
LRU_BLOCK = LRU_WIDTH // LRU_BLOCKS
LRU_C = 8.0
GDN_HEADS = 8
GDN_DK = 128
GDN_DV = 128
GDN_KEY = GDN_HEADS * GDN_DK
GDN_VAL = GDN_HEADS * GDN_DV
XA_HEADS = 4
XA_DH = D_MODEL // XA_HEADS
D_FF = 4 * D_MODEL

LANE = 128
VMEM_LIMIT = 56 * 1024 * 1024

GLA_LEVELS = (32, 16, 8, 4, 2, 1)
GDN_LEVELS = (2, 4, 8, 16, 32)

NT_DIMS = (((1,), (1,)), ((), ()))


def _params(n_axes):
    return pltpu.CompilerParams(
        dimension_semantics=("arbitrary",) * n_axes, vmem_limit_bytes=VMEM_LIMIT)


def _rms(x, g):
    ms = jnp.mean(x * x, axis=-1, keepdims=True)
    return x * lax.rsqrt(ms + EPS) * g


def _softplus(x):
    return jnp.maximum(x, 0.0) + jnp.log1p(jnp.exp(-jnp.abs(x)))


def _sigmoid(x):
    return 0.5 + 0.5 * jnp.tanh(0.5 * x)


def _silu(x):
    hx = 0.5 * x
    return hx + hx * jnp.tanh(hx)


def _dot(a, b):
    return jnp.dot(a, b, preferred_element_type=F32)


def _dot_nt(a, b):
    return lax.dot_general(a, b, NT_DIMS, preferred_element_type=F32)


def _split2(x):
    hi = x.astype(BF16)
    lo = (x - hi.astype(F32)).astype(BF16)
    return hi, lo


def _split3(x):
    hi = x.astype(BF16)
    r = x - hi.astype(F32)
    mid = r.astype(BF16)
    lo = (r - mid.astype(F32)).astype(BF16)
    return hi, mid, lo


def _shifted_taps(prev_tail, cur, cw):
    tm, c = cur.shape
    nt = tm // 8
    xt = jnp.concatenate([prev_tail, cur], axis=0).reshape(nt + 1, 8, c)
    sub = lax.broadcasted_iota(jnp.int32, (nt, 8, c), 1)
    y = cw[CONV_W - 1:CONV_W, :] * xt[1:]
    for kk in range(CONV_W - 1):
        s = CONV_W - 1 - kk
        r = pltpu.roll(xt, s, 1)
        y = y + cw[kk:kk + 1, :] * jnp.where(sub < s, r[:-1], r[1:])
    return y.reshape(tm, c)


def _norm_matmul_body(x_ref, g_ref, w_ref, *rest, n_chunk, conv, blocks_per_seq):
    if conv is None:
        (o_ref,) = rest
    else:
        c0, c1, use_silu, has_bias = conv
        if has_bias:
            cw_ref, cb_ref, o_ref, tail_ref = rest
        else:
            cw_ref, o_ref, tail_ref = rest

        @pl.when(pl.program_id(0) % blocks_per_seq == 0)
        def _():
            tail_ref[...] = jnp.zeros_like(tail_ref)

    tm = x_ref.shape[0]
    xn = _rms(x_ref[...], g_ref[...]).astype(BF16)
    n = o_ref.shape[1]
    starts = list(range(0, n, n_chunk))
    width = lambda c: min(n_chunk, n - c)
    res = _dot(xn, w_ref[:, 0:width(0)])
    for idx, c in enumerate(starts):
        cw = width(c)
        nxt = None
        if idx + 1 < len(starts):
            c2 = starts[idx + 1]
            nxt = _dot(xn, w_ref[:, c2:c2 + width(c2)])
        if conv is not None and c0 <= c < c1:
            lc = slice(c - c0, c - c0 + cw)
            y = _shifted_taps(tail_ref[:, lc], res, cw_ref[:, lc])
            if has_bias:
                y = y + cb_ref[:, lc]
            tail_ref[:, lc] = res[tm - 8:tm, :]
            res = _silu(y) if use_silu else y
        o_ref[:, c:c + cw] = res.astype(o_ref.dtype)
        res = nxt


def norm_matmul(x, g, w, layer, tm, conv=None, conv_w=None, conv_b=None, seq_len=None):
    m, d = x.shape
    n = w.shape[2]
    const = lambda i: (0, 0)
    in_specs = [pl.BlockSpec((tm, d), lambda i: (i, 0)),
                pl.BlockSpec((1, d), const),
                pl.BlockSpec((None, d, n), lambda i: (layer, 0, 0))]
    args = [x, g.reshape(1, d), w]
    scratch = []
    if conv is not None:
        width = conv[1] - conv[0]
        conv = (*conv, conv_b is not None)
        in_specs.append(pl.BlockSpec((CONV_W, width), const))
        args.append(conv_w)
        if conv_b is not None:
            in_specs.append(pl.BlockSpec((1, width), const))
            args.append(conv_b.reshape(1, width))
        scratch = [pltpu.VMEM((8, width), F32)]
    return pl.pallas_call(
        functools.partial(_norm_matmul_body, n_chunk=256, conv=conv,
                          blocks_per_seq=None if conv is None else seq_len // tm),
        grid=(m // tm,),
        in_specs=in_specs,
        out_specs=pl.BlockSpec((tm, n), lambda i: (i, 0)),
        out_shape=jax.ShapeDtypeStruct((m, n), BF16),
        scratch_shapes=scratch,
        compiler_params=_params(1),
        name="norm_matmul",
    )(*args)


def norm_matmul_all_layers(x, g, w, tm):
    m, d = x.shape
    layers, _, n = w.shape
    return pl.pallas_call(
        functools.partial(_norm_matmul_body, n_chunk=256, conv=None, blocks_per_seq=None),
        grid=(layers, m // tm),
        in_specs=[pl.BlockSpec((tm, d), lambda l, i: (i, 0)),
                  pl.BlockSpec((None, 1, d), lambda l, i: (l, 0, 0)),
                  pl.BlockSpec((None, d, n), lambda l, i: (l, 0, 0))],
        out_specs=pl.BlockSpec((None, tm, n), lambda l, i: (l, i, 0)),
        out_shape=jax.ShapeDtypeStruct((layers, m, n), BF16),
        compiler_params=_params(2),
        name="norm_matmul_all_layers",
    )(x, g.reshape(layers, 1, d), w)


def _xattn_block(h, kv_ref, wq_ref, wo_ref, pre_g, post_g):
    hn = _rms(h, pre_g).astype(BF16)
    q = _dot(hn, wq_ref[...])
    scale = XA_DH ** -0.5
    outs = []
    for hd in range(XA_HEADS):
        lo = hd * XA_DH
        qh = q[:, lo:lo + XA_DH].astype(BF16)
        kh = kv_ref[:, lo:lo + XA_DH]
        vh = kv_ref[:, D_MODEL + lo:D_MODEL + lo + XA_DH]
        s = _dot_nt(qh, kh) * scale
        p = jnp.exp(s - jnp.max(s, axis=-1, keepdims=True))
        l = jnp.sum(p, axis=-1, keepdims=True)
        outs.append((_dot(p.astype(BF16), vh) / l).astype(BF16))
    o = jnp.concatenate(outs, axis=-1)
    return h + _rms(_dot(o, wo_ref[...]), post_g)


def _mlp_block(h, w1_ref, w2_ref, pre_g, post_g, ff_chunk):
    hn = _rms(h, pre_g).astype(BF16)
    acc = jnp.zeros(h.shape, F32)
    for c in range(0, w1_ref.shape[1], ff_chunk):
        a = _dot(hn, w1_ref[:, c:c + ff_chunk])
        a = jnp.square(jnp.maximum(a, 0.0)).astype(BF16)
        acc = acc + _dot(a, w2_ref[c:c + ff_chunk, :])
    return h + _rms(acc, post_g)


def _post_body(*refs, n_mix, ff_chunk):
    a_refs = refs[:n_mix]
    wout_refs = refs[n_mix:2 * n_mix]
    (h_ref, kv_ref, wq_ref, wo_ref, w1_ref, w2_ref, gains_ref, o_ref) = refs[2 * n_mix:]
    gain = lambda r: gains_ref[r:r + 1, :]
    acc = _dot(a_refs[0][...], wout_refs[0][...])
    for a_ref, w_ref in zip(a_refs[1:], wout_refs[1:]):
        acc = acc + _dot(a_ref[...], w_ref[...])
    h = h_ref[...] + _rms(acc, gain(0))
    h = _xattn_block(h, kv_ref, wq_ref, wo_ref, gain(1), gain(2))
    o_ref[...] = _mlp_block(h, w1_ref, w2_ref, gain(3), gain(4), ff_chunk)


def post_mixer(mixed_list, w_out, out_layer, h, kv, w_q, w_o, w1, w2, layer, gains, seq_len, tm):
    m, d = h.shape
    n_mix = len(mixed_list)
    blocks_per_seq = seq_len // tm

    def resident(w, lyr, row_block=0, rows=None):
        rows = w.shape[1] if rows is None else rows
        return pl.BlockSpec((None, rows, w.shape[2]), lambda i: (lyr, row_block, 0),
                            pipeline_mode=pl.Buffered(1))

    in_specs = [pl.BlockSpec((tm, a.shape[1]), lambda i: (i, 0)) for a in mixed_list]
    in_specs += [resident(w_out, out_layer, k, a.shape[1]) for k, a in enumerate(mixed_list)]
    in_specs += [pl.BlockSpec((tm, d), lambda i: (i, 0)),
                 pl.BlockSpec((None, N_MEM, 2 * d), lambda i: (layer, i // blocks_per_seq, 0)),
                 resident(w_q, layer), resident(w_o, layer), resident(w1, layer), resident(w2, layer),
                 pl.BlockSpec(gains.shape, lambda i: (0, 0), pipeline_mode=pl.Buffered(1))]
    wout_list = [w_out] * n_mix
    return pl.pallas_call(
        functools.partial(_post_body, n_mix=n_mix, ff_chunk=512),
        grid=(m // tm,),
        in_specs=in_specs,
        out_specs=pl.BlockSpec((tm, d), lambda i: (i, 0)),
        out_shape=jax.ShapeDtypeStruct((m, d), F32),
        compiler_params=_params(1),
        name="post_mixer",
    )(*mixed_list, *wout_list, h, kv, w_q, w_o, w1, w2, gains)


def _gla_constants():
    c = CHUNK
    t = np.arange(c)
    i = t[:, None]
    tt = t[None, :]
    mats = [(tt <= i), (tt > i)]
    masks = []
    for s in GLA_LEVELS:
        sb = i // (2 * s)
        r = sb * 2 * s + s - 1
        right = (i % (2 * s)) >= s
        d = np.where(right, (tt > r) & (tt <= i), (tt > i) & (tt <= r))
        mats.append(d)
        j = tt
        pair = right & ((j // (2 * s)) == sb) & ((j % (2 * s)) < s)
        masks.append(pair)
    masks.append(i == tt)
    dall = np.concatenate(mats, axis=0).astype(np.float32)
    lvl = np.stack([np.tile(m, (1, GLA_HEADS)) for m in masks]).astype(np.float32)
    hk = np.arange(GLA_KEY) // GLA_DK
    hv = np.arange(GLA_VAL) // GLA_DV
    bd_kk = (hk[:, None] == hk[None, :]).astype(np.float32)
    bd_kv = (hk[:, None] == hv[None, :]).astype(np.float32)
    return (jnp.asarray(dall, BF16), jnp.asarray(lvl, F32), jnp.asarray(bd_kk, BF16),
            jnp.asarray(bd_kv, BF16), jnp.asarray(bd_kv, F32))


def _gla_body(qk_ref, v_ref, go_ref, lr_ref, w2_ref, b_ref, ng_ref, dall_ref, lvl_ref,
              bdkk_ref, bdkv_ref, bdkvf_ref, o_ref, st_ref, *, n_chunks):
    @pl.when(pl.program_id(1) == 0)
    def _():
        st_ref[...] = jnp.zeros_like(st_ref)

    dall = dall_ref[...]
    bdkk = bdkk_ref[...]
    bdkv = bdkv_ref[...]
    bdkv_f = bdkvf_ref[...]
    n_lvl = len(GLA_LEVELS)

    chunks = range(n_chunks)

    def rows(ci):
        return slice(ci * CHUNK, (ci + 1) * CHUNK)

    z = _dot(lr_ref[...], w2_ref[...]) + b_ref[...]
    gk = -_softplus(-z) * (1.0 / GLA_GATE_NORM)
    g_hi, g_lo = _split2(gk)
    e_all = [jnp.exp(jnp.minimum(_dot(dall, g_hi[rows(ci)]) + _dot(dall, g_lo[rows(ci)]), 0.0))
             for ci in chunks]
    q = [qk_ref[rows(ci), 0:GLA_KEY].astype(F32) * (GLA_DK ** -0.5) for ci in chunks]
    k = [qk_ref[rows(ci), GLA_KEY:2 * GLA_KEY].astype(F32) for ci in chunks]
    v = [v_ref[rows(ci), :] for ci in chunks]

    q_in = [(q[ci] * e_all[ci][0:CHUNK]).astype(BF16) for ci in chunks]
    k_up_t = [(k[ci] * e_all[ci][CHUNK:2 * CHUNK]).T.astype(BF16) for ci in chunks]
    d_col = [e_all[ci][CHUNK - 8:CHUNK].T[:, 7:8] for ci in chunks]
    kv = [bdkv_f * _dot(k_up_t[ci], v[ci]) for ci in chunks]

    a = [jnp.zeros((CHUNK, GLA_KEY), F32) for _ in chunks]
    for li in range(n_lvl + 1):
        for ci in chunks:
            if li < n_lvl:
                e = e_all[ci][(2 + li) * CHUNK:(3 + li) * CHUNK]
                qs, ks = q[ci] * e, k[ci] * e
            else:
                qs, ks = q[ci], k[ci]
            kbd = jnp.concatenate([ks.astype(BF16)] * GLA_HEADS, axis=0) * bdkk
            a[ci] = a[ci] + _dot_nt(qs.astype(BF16), kbd) * lvl_ref[li]
    o = [_dot(a[ci].astype(BF16), jnp.concatenate([v[ci]] * GLA_HEADS, axis=0) * bdkv) for ci in chunks]

    st = st_ref[...]
    for ci in chunks:
        o[ci] = o[ci] + _dot(q_in[ci], st.astype(BF16))
        st = st * d_col[ci] + kv[ci]
    st_ref[...] = st

    ng = ng_ref[...]
    o = jnp.concatenate(o, axis=0)
    outs = []
    for hd in range(GLA_HEADS):
        sl = slice(hd * GLA_DV, (hd + 1) * GLA_DV)
        outs.append(_rms(o[:, sl], ng) * _silu(go_ref[:, sl].astype(F32)))
    o_ref[...] = jnp.concatenate(outs, axis=-1).astype(o_ref.dtype)


def gla(proj, gk_w2, gk_b, norm_g, batch, seq_len, tb):
    m = proj.shape[0]
    nb = seq_len // tb
    dall, lvl, bdkk, bdkv, bdkv_f = _gla_constants()
    w2 = jnp.zeros((LANE, GLA_KEY), F32).at[:GLA_GATE_RANK].set(gk_w2).astype(BF16)
    row = lambda b, t: (b * nb + t, 0)
    const2 = lambda b, t: (0, 0)
    return pl.pallas_call(
        functools.partial(_gla_body, n_chunks=tb // CHUNK),
        grid=(batch, nb),
        in_specs=[pl.BlockSpec((tb, 2 * GLA_KEY), lambda b, t: (b * nb + t, 0)),
                  pl.BlockSpec((tb, GLA_VAL), lambda b, t: (b * nb + t, 1)),
                  pl.BlockSpec((tb, GLA_VAL), lambda b, t: (b * nb + t, 2)),
                  pl.BlockSpec((tb, LANE), lambda b, t: (b * nb + t, 20)),
                  pl.BlockSpec((LANE, GLA_KEY), const2),
                  pl.BlockSpec((1, GLA_KEY), const2),
                  pl.BlockSpec((1, GLA_DV), const2),
                  pl.BlockSpec(dall.shape, const2),
                  pl.BlockSpec(lvl.shape, lambda b, t: (0, 0, 0)),
                  pl.BlockSpec(bdkk.shape, const2),
                  pl.BlockSpec(bdkv.shape, const2),
                  pl.BlockSpec(bdkv_f.shape, const2)],
        out_specs=pl.BlockSpec((tb, GLA_VAL), row),
        out_shape=jax.ShapeDtypeStruct((m, GLA_VAL), BF16),
        scratch_shapes=[pltpu.VMEM((GLA_KEY, GLA_VAL), F32)],
        compiler_params=_params(2),
        name="gla",
    )(proj, proj, proj, proj, w2, gk_b.reshape(1, GLA_KEY), norm_g.reshape(1, GLA_DV),
      dall, lvl, bdkk, bdkv, bdkv_f)


def _gelu_tanh(x):
    c = np.float32(np.sqrt(2.0 / np.pi))
    return 0.5 * x * (1.0 + jnp.tanh(c * (x + 0.044715 * (x * x * x))))


def _lru_body(x_ref, gate_ref, wa_ref, ba_ref, wx_ref, bx_ref, lam_ref, y_ref, hc_ref, *, tb):
    @pl.when(pl.program_id(1) == 0)
    def _():
        hc_ref[...] = jnp.zeros_like(hc_ref)

    xb = x_ref[...]
    xr = xb.astype(F32)
    ra, ia = [], []
    for n in range(LRU_BLOCKS):
        sl = slice(n * LRU_BLOCK, (n + 1) * LRU_BLOCK)
        ra.append(_dot(xb[:, sl], wa_ref[n]))
        ia.append(_dot(xb[:, sl], wx_ref[n]))
    r = _sigmoid(jnp.concatenate(ra, axis=-1) + ba_ref[...])
    ig = _sigmoid(jnp.concatenate(ia, axis=-1) + bx_ref[...])
    log_a = (-LRU_C) * r * _softplus(-lam_ref[...])
    a = jnp.exp(log_a)
    u = jnp.sqrt(1.0 - jnp.exp(2.0 * log_a)) * (ig * xr)

    nt = tb // 8
    a3 = a.reshape(nt, 8, LRU_WIDTH)
    u3 = u.reshape(nt, 8, LRU_WIDTH)
    sub = lax.broadcasted_iota(jnp.int32, (nt, 8, LRU_WIDTH), 1)
    for s in (1, 2, 4):
        keep = sub >= s
        a_sh = jnp.where(keep, pltpu.roll(a3, s, 1), 1.0)
        u_sh = jnp.where(keep, pltpu.roll(u3, s, 1), 0.0)
        u3 = a3 * u_sh + u3
        a3 = a3 * a_sh
    carry = hc_ref[7:8, :]
    tiles = []
    for t in range(nt):
        tiles.append(a3[t] * carry + u3[t])
        carry = tiles[-1][7:8, :]
    h = jnp.concatenate(tiles, axis=0)
    hc_ref[...] = tiles[-1]
    y_ref[...] = (h * _gelu_tanh(gate_ref[...].astype(F32))).astype(y_ref.dtype)


def lru(proj, w_a, b_a, w_x, b_x, lam, batch, seq_len, tb):
    m = proj.shape[0]
    nb = seq_len // tb
    w = LRU_WIDTH
    const2 = lambda b, t: (0, 0)
    const3 = lambda b, t: (0, 0, 0)
    return pl.pallas_call(
        functools.partial(_lru_body, tb=tb),
        grid=(batch, nb),
        in_specs=[pl.BlockSpec((tb, w), lambda b, t: (b * nb + t, 3)),
                  pl.BlockSpec((tb, w), lambda b, t: (b * nb + t, 4)),
                  pl.BlockSpec((LRU_BLOCKS, LRU_BLOCK, LRU_BLOCK), const3),
                  pl.BlockSpec((1, w), const2),
                  pl.BlockSpec((LRU_BLOCKS, LRU_BLOCK, LRU_BLOCK), const3),
                  pl.BlockSpec((1, w), const2),
                  pl.BlockSpec((1, w), const2)],
        out_specs=pl.BlockSpec((tb, w), lambda b, t: (b * nb + t, 0)),
        out_shape=jax.ShapeDtypeStruct((m, w), BF16),
        scratch_shapes=[pltpu.VMEM((8, w), F32)],
        compiler_params=_params(2),
        name="lru",
    )(proj, proj, w_a.astype(BF16), b_a.reshape(1, w),
      w_x.astype(BF16), b_x.reshape(1, w), lam.reshape(1, w))


def _gdn_constants():
    c = CHUNK
    t = np.arange(c)
    i = t[:, None]
    j = t[None, :]
    csum = np.concatenate([(j <= i), (j > i)], axis=0).astype(np.float32)
    masks = [(i >= j), (i > j)]
    lv = []
    for s in (1,) + GDN_LEVELS:
        sb = i // (2 * s)
        pair = ((i % (2 * s)) >= s) & ((j // (2 * s)) == sb) & ((j % (2 * s)) < s)
        lv.append(pair)
    masks = np.stack(masks + lv).astype(np.float32)
    eye = np.eye(c, dtype=np.float32)
    half = np.arange(2 * GDN_DK) // GDN_DK
    ones_bd = (half[:, None] == half[None, :]).astype(np.float32)
    return jnp.asarray(csum, BF16), jnp.asarray(masks, F32), jnp.asarray(eye, F32), jnp.asarray(ones_bd, BF16)


def _gdn_body(q_ref, k_ref, v_ref, z_ref, bl_ref, al_ref, alog_ref, dtb_ref, ng_ref,
              csum_ref, masks_ref, eye_ref, ones_ref, o_ref, s_ref, *, tb):
    @pl.when(pl.program_id(1) == 0)
    def _():
        s_ref[...] = jnp.zeros_like(s_ref)

    beta_all = _sigmoid(bl_ref[...].astype(F32))
    g_all = -jnp.exp(alog_ref[...]) * _softplus(al_ref[...].astype(F32) + dtb_ref[...])
    csum = csum_ref[...]
    incl = masks_ref[0]
    strict = masks_ref[1]
    eye = eye_ref[...]
    ng = ng_ref[...]
    scale = GDN_DK ** -0.5

    n_chunks = tb // CHUNK
    heads = range(GDN_HEADS)
    items = [(ci, hd) for ci in range(n_chunks) for hd in heads]

    gcol, grow = [], []
    for ci in range(n_chunks):
        parts = _split3(g_all[ci * CHUNK:(ci + 1) * CHUNK, :])
        gcol.append(sum(_dot(csum, p) for p in parts))
        grow.append(gcol[ci][0:CHUNK, :].T)

    def rows(ci):
        return slice(ci * CHUNK, (ci + 1) * CHUNK)

    def head(ref, ci, hd):
        return ref[rows(ci), hd * GDN_DK:(hd + 1) * GDN_DK].astype(F32)

    ones_bd = ones_ref[...]
    qraw = {it: head(q_ref, *it) for it in items}
    kraw = {it: head(k_ref, *it) for it in items}
    ssq = {}
    for ci in range(n_chunks):
        sq = jnp.concatenate(
            [jnp.concatenate([qraw[ci, hd] * qraw[ci, hd], kraw[ci, hd] * kraw[ci, hd]], axis=1)
             for hd in heads], axis=0).astype(BF16)
        tot = _dot(sq, ones_bd)
        for hd in heads:
            ssq[ci, hd] = tot[hd * CHUNK:(hd + 1) * CHUNK]

    qn, kn, vb16, kb, eg, decay, kgt16, kk_qk = {}, {}, {}, {}, {}, {}, {}, {}
    for it in items:
        ci, hd = it
        qn[it] = qraw[it] * (lax.rsqrt(ssq[it][:, :GDN_DK] + EPS) * scale)
        kn[it] = kraw[it] * lax.rsqrt(ssq[it][:, GDN_DK:] + EPS)
        beta = beta_all[rows(ci), hd:hd + 1]
        gc = gcol[ci][0:CHUNK, hd:hd + 1]
        gs = gcol[ci][CHUNK:2 * CHUNK, hd:hd + 1]
        eg[it] = jnp.exp(gc)
        decay[it] = jnp.exp(jnp.minimum(gc - grow[ci][hd:hd + 1, :], 0.0)) * incl
        kb[it] = kn[it] * beta
        kk_qk[it] = _dot_nt(jnp.concatenate([kb[it], qn[it]], axis=0).astype(BF16), kn[it].astype(BF16))
        vb16[it] = (head(v_ref, ci, hd) * beta).astype(BF16)
        kgt16[it] = (kn[it] * jnp.exp(gs)).T.astype(BF16)
    m = {it: kk_qk[it][0:CHUNK] * (decay[it] * strict) for it in items}
    aqk16 = {it: (kk_qk[it][CHUNK:2 * CHUNK] * decay[it]).astype(BF16) for it in items}

    tinv = {it: eye - m[it] * masks_ref[2] for it in items}
    for li in range(len(GDN_LEVELS)):
        t16 = {it: tinv[it].astype(BF16) for it in items}
        p16 = {it: _dot(t16[it], (m[it] * masks_ref[3 + li]).astype(BF16)).astype(BF16) for it in items}
        tinv = {it: tinv[it] - _dot(p16[it], t16[it]) for it in items}
    uw = {it: _dot(tinv[it].astype(BF16),
                   jnp.concatenate([vb16[it], (kb[it] * eg[it]).astype(BF16)], axis=1)) for it in items}
    wq16 = {it: jnp.concatenate([uw[it][:, GDN_DV:], qn[it] * eg[it]], axis=0).astype(BF16) for it in items}

    s = {hd: s_ref[hd] for hd in heads}
    for ci in range(n_chunks):
        ws = {hd: _dot(wq16[ci, hd], s[hd].astype(BF16)) for hd in heads}
        vn16 = {hd: (uw[ci, hd][:, :GDN_DV] - ws[hd][0:CHUNK]).astype(BF16) for hd in heads}
        o = {hd: ws[hd][CHUNK:2 * CHUNK] + _dot(aqk16[ci, hd], vn16[hd]) for hd in heads}
        s = {hd: s[hd] * jnp.exp(gcol[ci][CHUNK - 1:CHUNK, hd:hd + 1]) + _dot(kgt16[ci, hd], vn16[hd])
             for hd in heads}
        for hd in heads:
            sl = slice(hd * GDN_DV, (hd + 1) * GDN_DV)
            gate = _silu(z_ref[rows(ci), sl].astype(F32))
            o_ref[rows(ci), sl] = (_rms(o[hd], ng) * gate).astype(o_ref.dtype)
    for hd in heads:
        s_ref[hd] = s[hd]


def gdn(proj, a_log, dt_bias, norm_g, batch, seq_len, tb):
    m = proj.shape[0]
    nb = seq_len // tb
    csum, masks, eye, ones_bd = _gdn_constants()
    pad8 = lambda vec: jnp.zeros((1, LANE), F32).at[0, :GDN_HEADS].set(vec)
    const2 = lambda b, t: (0, 0)
    col = lambda c: (lambda b, t: (b * nb + t, c))
    nk = GDN_KEY // LANE
    return pl.pallas_call(
        functools.partial(_gdn_body, tb=tb),
        grid=(batch, nb),
        in_specs=[pl.BlockSpec((tb, GDN_KEY), col(0)),
                  pl.BlockSpec((tb, GDN_KEY), col(1)),
                  pl.BlockSpec((tb, GDN_VAL), col(2)),
                  pl.BlockSpec((tb, GDN_VAL), col(3)),
                  pl.BlockSpec((tb, LANE), col(4 * nk)),
                  pl.BlockSpec((tb, LANE), col(4 * nk + 1)),
                  pl.BlockSpec((1, LANE), const2),
                  pl.BlockSpec((1, LANE), const2),
                  pl.BlockSpec((1, GDN_DV), const2),
                  pl.BlockSpec(csum.shape, const2),
                  pl.BlockSpec(masks.shape, lambda b, t: (0, 0, 0)),
                  pl.BlockSpec(eye.shape, const2),
                  pl.BlockSpec(ones_bd.shape, const2)],
        out_specs=pl.BlockSpec((tb, GDN_VAL), col(0)),
        out_shape=jax.ShapeDtypeStruct((m, GDN_VAL), BF16),
        scratch_shapes=[pltpu.VMEM((GDN_HEADS, GDN_DK, GDN_DV), F32)],
        compiler_params=_params(2),
        name="gdn",
    )(proj, proj, proj, proj, proj, proj, pad8(a_log), pad8(dt_bias),
      norm_g.reshape(1, GDN_DV), csum, masks, eye, ones_bd)


def _pad_cols(w, width):
    return jnp.pad(w, ((0, 0),) * (w.ndim - 1) + ((0, width - w.shape[-1]),))


def _even_w_in(w):
    o1 = GLA_KEY
    o2 = o1 + GLA_KEY
    o3 = o2 + GLA_VAL
    o4 = o3 + GLA_VAL
    o5 = o4 + GLA_GATE_RANK
    w = w.astype(BF16)
    return jnp.concatenate([w[..., :o4], w[..., o5:], _pad_cols(w[..., o4:o5], LANE)], axis=-1)


def _odd_w_in(w):
    p2 = 3 * GDN_KEY + GDN_VAL
    p3 = p2 + GDN_HEADS
    w = w.astype(BF16)
    return jnp.concatenate([w[..., :p2], _pad_cols(w[..., p2:p3], LANE), _pad_cols(w[..., p3:], LANE)], axis=-1)


def _gains(*rows):
    g = jnp.stack(rows)
    return jnp.pad(g, ((0, 8 - g.shape[0]), (0, 0)))


def even_mixer(h, pre_g, w_in, layer, gk_w2, gk_b, gla_g, conv_w, conv_b, w_a, b_a, w_x, b_x, lam,
               batch, seq_len, tm, tb):
    lru_x0 = 2 * GLA_KEY + 2 * GLA_VAL
    proj = norm_matmul(h, pre_g, w_in, layer, tm, conv=(lru_x0, lru_x0 + LRU_WIDTH, False),
                       conv_w=conv_w, conv_b=conv_b, seq_len=seq_len)
    o = gla(proj, gk_w2, gk_b, gla_g, batch, seq_len, tb)
    y = lru(proj, w_a, b_a, w_x, b_x, lam, batch, seq_len, tb)
    return [o, y]


def odd_mixer(h, pre_g, w_in, layer, conv_w, a_log, dt_bias, norm_g, batch, seq_len, tm, tb):
    n_conv = 2 * GDN_KEY + GDN_VAL
    proj = norm_matmul(h, pre_g, w_in, layer, tm, conv=(0, n_conv, True), conv_w=conv_w, seq_len=seq_len)
    return [gdn(proj, a_log, dt_bias, norm_g, batch, seq_len, tb)]


def kernel(x, mem, mix_pre_g, mix_post_g, ab_w_in, gla_gk_w2, gla_gk_b, gla_norm_g, lru_conv_w, lru_conv_b, lru_w_a, lru_b_a, lru_w_x, lru_b_x, lru_lambda, ab_w_out, gdn_w_in, gdn_conv_w, gdn_a_log, gdn_dt_bias, gdn_norm_g, gdn_w_out, xa_pre_g, xa_mem_g, xa_post_g, xa_w_q, xa_w_kv, xa_w_o, mlp_pre_g, mlp_post_g, mlp_w1, mlp_w2):
    batch, seq_len, d = x.shape
    depth = mix_pre_g.shape[0]
    tm = min(512, seq_len)
    tb_even = min(256, seq_len)
    tb_odd = min(256, seq_len)
    h = x.reshape(batch * seq_len, d)
    mem2 = mem.reshape(batch * mem.shape[1], d)
    even_w_in = _even_w_in(ab_w_in)
    odd_w_in = _odd_w_in(gdn_w_in)
    even_w_out = ab_w_out.astype(BF16)
    odd_w_out = gdn_w_out.astype(BF16)
    w_q, w_kv, w_o = xa_w_q.astype(BF16), xa_w_kv.astype(BF16), xa_w_o.astype(BF16)
    w1, w2 = mlp_w1.astype(BF16), mlp_w2.astype(BF16)
    kv = norm_matmul_all_layers(mem2, xa_mem_g, w_kv, min(512, mem2.shape[0]))
    for i in range(depth):
        j = i // 2
        if i % 2 == 0:
            mixed = even_mixer(h, mix_pre_g[i], even_w_in, j, gla_gk_w2[j], gla_gk_b[j], gla_norm_g[j],
                               lru_conv_w[j], lru_conv_b[j], lru_w_a[j], lru_b_a[j], lru_w_x[j], lru_b_x[j],
                               lru_lambda[j], batch, seq_len, tm, tb_even)
            w_out = even_w_out
        else:
            mixed = odd_mixer(h, mix_pre_g[i], odd_w_in, j, gdn_conv_w[j], gdn_a_log[j], gdn_dt_bias[j],
                              gdn_norm_g[j], batch, seq_len, tm, tb_odd)
            w_out = odd_w_out
        gains = _gains(mix_post_g[i], xa_pre_g[i], xa_post_g[i], mlp_pre_g[i], mlp_post_g[i])
        h = post_mixer(mixed, w_out, j, h, kv, w_q, w_o, w1, w2, i, gains, seq_len, tm)
    return h.reshape(batch, seq_len, d)
```
